```python
import jax
import jax.numpy as jnp
from jax import lax
import numpy as np

D_MODEL = 1024
BATCH = 4
SEQ = 4096
DEPTH = 4

HEAD_DIM = 64
N_GDN = 6
N_NSA = 6
N_NSA_KV = 2
NSA_REP = N_NSA // N_NSA_KV
N_SB = 4
D_GDN = N_GDN * HEAD_DIM
D_NSA = N_NSA * HEAD_DIM
D_NSA_KV = N_NSA_KV * HEAD_DIM
D_SB = N_SB * HEAD_DIM
D_MIX = D_GDN + D_NSA + D_SB
GDN_CONV = 4
GDN_CHUNK = 64
CMP_LEN = 32
CMP_STRIDE = 16
CMP_HIDDEN = 128
SEL_BLOCK = 64
SEL_TOPK = 16
WINDOW = 512
Q_BLOCK = 128
D_FF = 2816
FFN_CONV = 3
D_PLE = 256
ROPE_THETA = 10000.0
EPS = 1e-6
NEG = -1e30
FORCE_SCORE = 1e4
SPLIT_SIZES = (D_GDN, D_GDN, D_GDN, D_GDN, N_GDN, N_GDN,
               D_NSA, D_NSA_KV, D_NSA_KV, D_NSA_KV, D_NSA_KV, D_NSA_KV, D_NSA_KV, 3 * N_NSA,
               D_SB, D_SB, D_SB)
SPLIT_POINTS = tuple(int(c) for c in np.cumsum(SPLIT_SIZES)[:-1])
N_IN = int(sum(SPLIT_SIZES))

kernel_name = 'hybrid_gdn_nsa_stickbreaking_convffn_trunk'


def rmsnorm(x, w):
    xf = x.astype(jnp.float32)
    y = xf * lax.rsqrt(jnp.mean(xf * xf, axis=-1, keepdims=True) + EPS)
    return (y * w.astype(jnp.float32)).astype(x.dtype)


def head_rmsnorm(x, w):
    b, t, c = x.shape
    return rmsnorm(x.reshape(b, t, c // HEAD_DIM, HEAD_DIM), w).reshape(b, t, c)


def l2norm(x):
    return x * lax.rsqrt(jnp.sum(x * x, axis=-1, keepdims=True) + EPS)


def causal_dwconv(x, w):
    k, c = w.shape
    return lax.conv_general_dilated(x, w[:, None, :].astype(x.dtype), window_strides=(1,),
                                    padding=[(k - 1, 0)], dimension_numbers=('NWC', 'WIO', 'NWC'),
                                    feature_group_count=c)


def rope(x, pos):
    half = x.shape[-1] // 2
    inv = ROPE_THETA ** (-jnp.arange(half, dtype=jnp.float32) / half)
    ang = pos.astype(jnp.float32)[..., None] * inv
    cos = jnp.cos(ang)[:, :, None, :]
    sin = jnp.sin(ang)[:, :, None, :]
    xf = x.astype(jnp.float32)
    x1, x2 = xf[..., :half], xf[..., half:]
    return jnp.concatenate([x1 * cos - x2 * sin, x2 * cos + x1 * sin], axis=-1).astype(x.dtype)


def masked_softmax(s, mask):
    s = jnp.where(mask, s, NEG)
    e = jnp.where(mask, jnp.exp(s - jnp.max(s, axis=-1, keepdims=True)), 0.0)
    return e / jnp.maximum(jnp.sum(e, axis=-1, keepdims=True), 1e-30)


def gated_deltanet(q, k, v, z, a, b_logit, conv_w, a_log, dt_bias, norm_w):
    bsz, t, _ = q.shape
    H, D, C = N_GDN, HEAD_DIM, GDN_CHUNK
    n = t // C
    f32 = jnp.float32
    qkv = jax.nn.silu(causal_dwconv(jnp.concatenate([q, k, v], axis=-1), conv_w))
    q, k, v = jnp.split(qkv, 3, axis=-1)
    to_chunks = lambda u_: u_.astype(f32).reshape(bsz, n, C, H, D).transpose(0, 3, 1, 2, 4)
    q = l2norm(to_chunks(q)) * D ** -0.5
    k = l2norm(to_chunks(k))
    v = to_chunks(v)
    beta = jax.nn.sigmoid(b_logit.astype(f32)).reshape(bsz, n, C, H).transpose(0, 3, 1, 2)
    g = -jnp.exp(a_log.astype(f32)) * jax.nn.softplus(a.astype(f32) + dt_bias.astype(f32))
    g = jnp.cumsum(g.reshape(bsz, n, C, H).transpose(0, 3, 1, 2), axis=-1)
    idx = jnp.arange(C)
    incl = idx[:, None] >= idx[None, :]
    strict = idx[:, None] > idx[None, :]
    decay = jnp.exp(jnp.where(incl, g[..., :, None] - g[..., None, :], -jnp.inf))
    k_beta = k * beta[..., None]
    lower = jnp.where(strict, jnp.einsum('bhncd,bhnsd->bhncs', k_beta, k) * decay, 0.0)
    rhs = jnp.concatenate([v * beta[..., None], k_beta * jnp.exp(g)[..., None]], axis=-1)
    sol = lax.linalg.triangular_solve(lower, rhs, left_side=True, lower=True, unit_diagonal=True)
    u, w = jnp.split(sol, 2, axis=-1)
    attn = jnp.einsum('bhncd,bhnsd->bhncs', q, k) * decay

    def step(state, xs):
        q_i, k_i, u_i, w_i, g_i, a_i = xs
        v_new = u_i - jnp.einsum('bhcd,bhde->bhce', w_i, state)
        o_i = (jnp.einsum('bhcd,bhde->bhce', q_i * jnp.exp(g_i)[..., None], state)
               + jnp.einsum('bhcs,bhse->bhce', a_i, v_new))
        g_last = g_i[..., -1:]
        state = (state * jnp.exp(g_last)[..., None]
                 + jnp.einsum('bhcd,bhce->bhde', k_i * jnp.exp(g_last - g_i)[..., None], v_new))
        return state, o_i

    xs = tuple(jnp.moveaxis(t_, 2, 0) for t_ in (q, k, u, w, g, attn))
    _, o = lax.scan(step, jnp.zeros((bsz, H, D, D), f32), xs)
    o = o.transpose(1, 0, 3, 2, 4).reshape(bsz, t, H, D)
    o = rmsnorm(o, norm_w) * jax.nn.silu(z.astype(f32).reshape(bsz, t, H, D))
    return o.reshape(bsz, t, D_GDN).astype(z.dtype)


def compress(kv, pe, w1, w2):
    b, t, g, d = kv.shape
    n_cmp = (t - CMP_LEN) // CMP_STRIDE + 1
    idx = jnp.arange(n_cmp)[:, None] * CMP_STRIDE + jnp.arange(CMP_LEN)[None, :]
    blocks = kv[:, idx] + pe[None, None, :, None, :]
    blocks = blocks.transpose(0, 1, 3, 2, 4).reshape(b, n_cmp, g, CMP_LEN * d)
    return jax.nn.silu(blocks @ w1) @ w2


def native_sparse_attention(q, k_cmp, v_cmp, k_slc, v_slc, k_win, v_win, gates, positions,
                            pe_k, pe_v, ck_w1, ck_w2, cv_w1, cv_w2):
    b, t, _ = q.shape
    G, R, D = N_NSA_KV, NSA_REP, HEAD_DIM
    scale = D ** -0.5
    f32 = jnp.float32
    q = rope(q.reshape(b, t, N_NSA, D), positions).reshape(b, t, G, R, D)
    kv = lambda z_: z_.reshape(b, t, G, D)
    kc = compress(rope(kv(k_cmp), positions), pe_k, ck_w1, ck_w2)
    vc = compress(kv(v_cmp), pe_v, cv_w1, cv_w2)
    n_cmp = kc.shape[1]
    n_sel = t // SEL_BLOCK
    top_k = min(SEL_TOPK, n_sel)
    ks_blk = rope(kv(k_slc), positions).reshape(b, n_sel, SEL_BLOCK, G, D).transpose(0, 3, 1, 2, 4)
    vs_blk = kv(v_slc).reshape(b, n_sel, SEL_BLOCK, G, D).transpose(0, 3, 1, 2, 4)
    pad = ((0, 0), (WINDOW, 0), (0, 0), (0, 0))
    kw_pad = jnp.pad(rope(kv(k_win), positions), pad)
    vw_pad = jnp.pad(kv(v_win), pad)
    c0 = jnp.arange(n_cmp) * CMP_STRIDE
    s0 = jnp.arange(n_sel) * SEL_BLOCK
    overlap = jnp.clip(jnp.minimum(c0[:, None] + CMP_LEN, s0[None, :] + SEL_BLOCK)
                       - jnp.maximum(c0[:, None], s0[None, :]), 0).astype(f32) / CMP_LEN
    cmp_end = c0 + CMP_LEN - 1
    b_idx = jnp.arange(b)[:, None, None, None]
    g_idx = jnp.arange(G)[None, :, None, None]
    blk = jnp.arange(n_sel)

    def block(args):
        qb, gb, q0 = args
        tq = q0 + jnp.arange(Q_BLOCK)
        s_c = jnp.einsum('bqgrd,bigd->bgrqi', qb, kc).astype(f32) * scale
        p_c = masked_softmax(s_c, cmp_end[None, :] <= tq[:, None])
        o_c = jnp.einsum('bgrqi,bigd->bqgrd', p_c.astype(vc.dtype), vc)
        imp = jnp.einsum('bgrqi,ij->bgqj', p_c, overlap)
        cur = tq // SEL_BLOCK
        forced = (blk[None, :] == 0) | (blk[None, :] == cur[:, None]) | (blk[None, :] == cur[:, None] - 1)
        visible = blk[None, :] * SEL_BLOCK <= tq[:, None]
        score = jnp.where(forced, FORCE_SCORE, jnp.where(visible, imp, -1.0))
        _, sel = lax.top_k(score, top_k)
        k_sel = ks_blk[b_idx, g_idx, sel].reshape(b, G, Q_BLOCK, top_k * SEL_BLOCK, D)
        v_sel = vs_blk[b_idx, g_idx, sel].reshape(b, G, Q_BLOCK, top_k * SEL_BLOCK, D)
        key_pos = (sel[..., None] * SEL_BLOCK + jnp.arange(SEL_BLOCK)).reshape(b, G, 1, Q_BLOCK, top_k * SEL_BLOCK)
        s_s = jnp.einsum('bqgrd,bgqnd->bgrqn', qb, k_sel).astype(f32) * scale
        p_s = masked_softmax(s_s, key_pos <= tq[:, None])
        o_s = jnp.einsum('bgrqn,bgqnd->bqgrd', p_s.astype(v_sel.dtype), v_sel)
        kw = lax.dynamic_slice_in_dim(kw_pad, q0, WINDOW + Q_BLOCK, axis=1)
        vw = lax.dynamic_slice_in_dim(vw_pad, q0, WINDOW + Q_BLOCK, axis=1)
        pos_w = q0 - WINDOW + jnp.arange(WINDOW + Q_BLOCK)
        diff = tq[:, None] - pos_w[None, :]
        m_w = (diff >= 0) & (diff < WINDOW) & (pos_w >= 0)[None, :]
        s_w = jnp.einsum('bqgrd,bngd->bgrqn', qb, kw).astype(f32) * scale
        p_w = masked_softmax(s_w, m_w)
        o_w = jnp.einsum('bgrqn,bngd->bqgrd', p_w.astype(vw.dtype), vw)
        gt = jax.nn.sigmoid(gb.astype(f32))[..., None]
        return (gt[:, :, 0] * o_c + gt[:, :, 1] * o_s + gt[:, :, 2] * o_w).astype(qb.dtype)

    nb = t // Q_BLOCK
    q_blocks = q.reshape(b, nb, Q_BLOCK, G, R, D).transpose(1, 0, 2, 3, 4, 5)
    g_blocks = gates.reshape(b, nb, Q_BLOCK, 3, G, R).transpose(1, 0, 2, 3, 4, 5)
    out = lax.map(block, (q_blocks, g_blocks, jnp.arange(nb) * Q_BLOCK))
    return out.transpose(1, 0, 2, 3, 4, 5).reshape(b, t, D_NSA)


def stick_breaking_attention(q, k, v):
    b, t, _ = q.shape
    H, D = N_SB, HEAD_DIM
    scale = D ** -0.5
    k = k.reshape(b, t, H, D)
    v = v.reshape(b, t, H, D)
    nb = t // Q_BLOCK
    q_blocks = q.reshape(b, nb, Q_BLOCK, H, D).transpose(1, 0, 2, 3, 4)
    key_pos = jnp.arange(t)

    def block(args):
        qb, q0 = args
        tq = q0 + jnp.arange(Q_BLOCK)
        strict = key_pos[None, :] < tq[:, None]
        z = jnp.einsum('bqhd,bkhd->bhqk', qb, k).astype(jnp.float32) * scale
        log_1m = jnp.where(strict, jax.nn.log_sigmoid(-z), 0.0)
        log_stick = lax.cumsum(log_1m, axis=3, reverse=True) - log_1m
        a = jnp.where(strict, jnp.exp(jax.nn.log_sigmoid(z) + log_stick), 0.0)
        return jnp.einsum('bhqk,bkhd->bqhd', a.astype(v.dtype), v)

    out = lax.map(block, (q_blocks, jnp.arange(nb) * Q_BLOCK))
    return out.transpose(1, 0, 2, 3, 4).reshape(b, t, D_SB)


def conv_ffn(x, w_up, conv_w, w_down):
    u = causal_dwconv(x @ w_up, conv_w)
    gate, up = jnp.split(u, 2, axis=-1)
    return (jax.nn.silu(gate) * up) @ w_down


def setup_inputs(seed: int = 0) -> dict:
    key = jax.random.key(seed)
    keys = iter(jax.random.split(key, 40))
    f32 = jnp.float32

    def normal(shape, scale):
        return jax.random.normal(next(keys), shape, f32) * scale

    def gain(shape):
        return 1.0 + 0.01 * jax.random.normal(next(keys), shape, f32)

    x = normal((BATCH, SEQ, D_MODEL), 1.0)
    p = normal((DEPTH, BATCH, SEQ, D_PLE), 1.0)
    positions = (jax.random.randint(next(keys), (BATCH, 1), 0, 1024, jnp.int32)
                 + jnp.arange(SEQ, dtype=jnp.int32)[None, :])
    a_log = jnp.log(jax.random.uniform(next(keys), (DEPTH, N_GDN), f32, 1.0, 16.0))
    dt = jnp.exp(jax.random.uniform(next(keys), (DEPTH, N_GDN), f32, -6.907755, -2.302585))
    dt_bias = dt + jnp.log(-jnp.expm1(-dt))
    return {
        'x': x,
        'p': p,
        'positions': positions,
        'ln_mix': gain((DEPTH, D_MODEL)),
        'w_in': normal((DEPTH, D_MODEL, N_IN), D_MODEL ** -0.5),
        'gdn_conv': normal((DEPTH, GDN_CONV, 3 * D_GDN), GDN_CONV ** -0.5),
        'gdn_a_log': a_log,
        'gdn_dt_bias': dt_bias,
        'gdn_norm': gain((DEPTH, HEAD_DIM)),
        'nsa_pe_k': normal((DEPTH, CMP_LEN, HEAD_DIM), 0.1),
        'nsa_pe_v': normal((DEPTH, CMP_LEN, HEAD_DIM), 0.1),
        'nsa_cmp_k_w1': normal((DEPTH, CMP_LEN * HEAD_DIM, CMP_HIDDEN), (CMP_LEN * HEAD_DIM) ** -0.5),
        'nsa_cmp_k_w2': normal((DEPTH, CMP_HIDDEN, HEAD_DIM), CMP_HIDDEN ** -0.5),
        'nsa_cmp_v_w1': normal((DEPTH, CMP_LEN * HEAD_DIM, CMP_HIDDEN), (CMP_LEN * HEAD_DIM) ** -0.5),
        'nsa_cmp_v_w2': normal((DEPTH, CMP_HIDDEN, HEAD_DIM), CMP_HIDDEN ** -0.5),
        'nsa_norm': gain((DEPTH, HEAD_DIM)),
        'sb_norm': gain((DEPTH, HEAD_DIM)),
        'w_out': normal((DEPTH, D_MIX, D_MODEL), D_MIX ** -0.5),
        'ln_ffn': gain((DEPTH, D_MODEL)),
        'w_up': normal((DEPTH, D_MODEL, 2 * D_FF), D_MODEL ** -0.5),
        'ffn_conv': normal((DEPTH, FFN_CONV, 2 * D_FF), FFN_CONV ** -0.5),
        'w_down': normal((DEPTH, D_FF, D_MODEL), D_FF ** -0.5),
        'ln_ple': gain((DEPTH, D_MODEL)),
        'w_ple_gate': normal((DEPTH, D_MODEL, D_MODEL), D_MODEL ** -0.5),
        'w_ple': normal((DEPTH, D_PLE, D_MODEL), D_PLE ** -0.5),
        'ple_norm': gain((DEPTH, D_MODEL)),
        'ln_final': gain((D_MODEL,)),
    }


def reference(x, p, positions, ln_mix, w_in, gdn_conv, gdn_a_log, gdn_dt_bias, gdn_norm,
              nsa_pe_k, nsa_pe_v, nsa_cmp_k_w1, nsa_cmp_k_w2, nsa_cmp_v_w1, nsa_cmp_v_w2,
              nsa_norm, sb_norm, w_out, ln_ffn, w_up, ffn_conv, w_down,
              ln_ple, w_ple_gate, w_ple, ple_norm, ln_final):
    h = x
    for i in range(DEPTH):
        hn = rmsnorm(h, ln_mix[i])
        (g_q, g_k, g_v, g_z, g_a, g_b,
         n_q, n_kc, n_vc, n_ks, n_vs, n_kw, n_vw, n_gate,
         s_q, s_k, s_v) = jnp.split(hn @ w_in[i], SPLIT_POINTS, axis=-1)
        o_gdn = gated_deltanet(g_q, g_k, g_v, g_z, g_a, g_b, gdn_conv[i], gdn_a_log[i],
                               gdn_dt_bias[i], gdn_norm[i])
        o_nsa = native_sparse_attention(n_q, n_kc, n_vc, n_ks, n_vs, n_kw, n_vw, n_gate, positions,
                                        nsa_pe_k[i], nsa_pe_v[i], nsa_cmp_k_w1[i], nsa_cmp_k_w2[i],
                                        nsa_cmp_v_w1[i], nsa_cmp_v_w2[i])
        o_sb = stick_breaking_attention(s_q, s_k, s_v)
        mix = jnp.concatenate([o_gdn, head_rmsnorm(o_nsa, nsa_norm[i]), head_rmsnorm(o_sb, sb_norm[i])], axis=-1)
        h = h + mix @ w_out[i]
        h = h + conv_ffn(rmsnorm(h, ln_ffn[i]), w_up[i], ffn_conv[i], w_down[i])
        gate = jax.nn.sigmoid(rmsnorm(h, ln_ple[i]) @ w_ple_gate[i])
        h = h + gate * rmsnorm(p[i] @ w_ple[i], ple_norm[i])
    return rmsnorm(h, ln_final)
```

```python
import functools

import jax
import jax.numpy as jnp
import numpy as np
from jax import lax
from jax.experimental import pallas as pl
from jax.experimental.pallas import tpu as pltpu

F32 = jnp.float32
BF16 = jnp.bfloat16
HIGHEST = lax.Precision.HIGHEST

LANES = 128
HEAD_DIM = 64
N_GDN = 6
N_NSA = 6
N_NSA_KV = 2
NSA_REP = N_NSA // N_NSA_KV
N_SB = 4
D_GDN = N_GDN * HEAD_DIM
D_NSA = N_NSA * HEAD_DIM
D_SB = N_SB * HEAD_DIM
GDN_CONV = 4
GDN_BLOCK = 128
CMP_LEN = 32
CMP_STRIDE = 16
SEL_BLOCK = 64
SEL_TOPK = 16
WINDOW = 512
Q_BLOCK = 128
FFN_CONV = 3
ROPE_THETA = 10000.0
EPS = 1e-6
NEG = -1e30
FORCE_SCORE = 1e4
EXP_UNDERFLOW = -104.0

CB_GQ, CB_GK, CB_GV, CB_GZ = 0, 3, 6, 9
CB_NQ = 12
CB_SMALL = 15
CB_KC, CB_VC, CB_KS, CB_VS, CB_KW, CB_VW = 16, 17, 18, 19, 20, 21
CB_SQ, CB_SK, CB_SV = 22, 24, 26
N_CB = 28
NP_IN = N_CB * LANES
SMALL_A, SMALL_B, SMALL_GATE = 0, 6, 12

NT_DIMS = (((1,), (1,)), ((), ()))
TN_DIMS = (((0,), (0,)), ((), ()))


def _cparams(n_axes, vmem_mb):
    return pltpu.CompilerParams(dimension_semantics=("arbitrary",) * n_axes,
                                vmem_limit_bytes=vmem_mb * 1024 * 1024)


def _inproj_columns():
    d_gdn, d_nsa, d_kv, d_sb = D_GDN, D_NSA, N_NSA_KV * HEAD_DIM, D_SB
    o = {}
    off = 0
    for name, size in (("gq", d_gdn), ("gk", d_gdn), ("gv", d_gdn), ("gz", d_gdn), ("ga", N_GDN), ("gb", N_GDN),
                       ("nq", d_nsa), ("kc", d_kv), ("vc", d_kv), ("ks", d_kv), ("vs", d_kv), ("kw", d_kv),
                       ("vw", d_kv), ("ng", 3 * N_NSA), ("sq", d_sb), ("sk", d_sb), ("sv", d_sb)):
        o[name] = off
        off += size
    cols = -np.ones((NP_IN,), np.int64)

    def put(cb, lane, src, size):
        cols[cb * LANES + lane: cb * LANES + lane + size] = np.arange(src, src + size)

    put(CB_GQ, 0, o["gq"], d_gdn)
    put(CB_GK, 0, o["gk"], d_gdn)
    put(CB_GV, 0, o["gv"], d_gdn)
    put(CB_GZ, 0, o["gz"], d_gdn)
    for r in range(NSA_REP):
        for g in range(N_NSA_KV):
            put(CB_NQ + r, g * HEAD_DIM, o["nq"] + (g * NSA_REP + r) * HEAD_DIM, HEAD_DIM)
    put(CB_SMALL, SMALL_A, o["ga"], N_GDN)
    put(CB_SMALL, SMALL_B, o["gb"], N_GDN)
    put(CB_SMALL, SMALL_GATE, o["ng"], 3 * N_NSA)
    for cb, nm in ((CB_KC, "kc"), (CB_VC, "vc"), (CB_KS, "ks"), (CB_VS, "vs"), (CB_KW, "kw"), (CB_VW, "vw")):
        put(cb, 0, o[nm], d_kv)
    put(CB_SQ, 0, o["sq"], d_sb)
    put(CB_SK, 0, o["sk"], d_sb)
    put(CB_SV, 0, o["sv"], d_sb)
    return cols


def _mix_rows():
    rows = list(range(D_GDN))
    for c in range(D_NSA):
        r, half, d = c // LANES, (c % LANES) // HEAD_DIM, c % HEAD_DIM
        rows.append(D_GDN + (half * NSA_REP + r) * HEAD_DIM + d)
    rows += list(range(D_GDN + D_NSA, D_GDN + D_NSA + D_SB))
    return np.asarray(rows, np.int64)


def _iota(shape, dim):
    return lax.broadcasted_iota(jnp.int32, shape, dim)


def _silu(x):
    return x * jax.nn.sigmoid(x)


def _head_sum(x, lo):
    s0 = jnp.sum(jnp.where(lo, x, 0.0), axis=1, keepdims=True)
    s1 = jnp.sum(jnp.where(lo, 0.0, x), axis=1, keepdims=True)
    return jnp.where(lo, s0, s1)


def _head_rmsnorm(x, w_row, lo):
    return x * lax.rsqrt(_head_sum(x * x, lo) * (1.0 / HEAD_DIM) + EPS) * w_row


def _rmsnorm_rows(x, w_row):
    return x * lax.rsqrt(jnp.mean(x * x, axis=-1, keepdims=True) + EPS) * w_row


def _inproj_kernel(x_ref, g_ref, w_ref, o_ref, xn_ref):
    @pl.when(pl.program_id(1) == 0)
    def _():
        xn_ref[...] = _rmsnorm_rows(x_ref[...], g_ref[...]).astype(BF16)

    o_ref[...] = jnp.dot(xn_ref[...], w_ref[...], preferred_element_type=F32)


def _inproj(h, gain, w_bf16, tm=512, tn=512):
    n, d = h.shape
    return pl.pallas_call(
        _inproj_kernel,
        grid=(n // tm, NP_IN // tn),
        in_specs=[pl.BlockSpec((tm, d), lambda i, j: (i, 0)),
                  pl.BlockSpec((1, d), lambda i, j: (0, 0)),
                  pl.BlockSpec((d, tn), lambda i, j: (0, j))],
        out_specs=pl.BlockSpec((tm, tn), lambda i, j: (i, j)),
        out_shape=jax.ShapeDtypeStruct((n, NP_IN), F32),
        scratch_shapes=[pltpu.VMEM((tm, d), BF16)],
        compiler_params=_cparams(2, 40),
        name="inproj",
    )(h, gain, w_bf16)


def _gdn_kernel(q_ref, k_ref, v_ref, z_ref, s_ref, cq_ref, ck_ref, cv_ref, alog_ref, dt_ref, nw_ref,
                o_ref, qp, kp, vp):
    hp = pl.program_id(1)
    t_len = q_ref.shape[0]
    blk = GDN_BLOCK
    pad = 8
    for src, dst in ((q_ref, qp), (k_ref, kp), (v_ref, vp)):
        dst[pl.ds(0, pad), :] = jnp.zeros((pad, LANES), F32)
        dst[pl.ds(pad, t_len), :] = src[...]

    ri = _iota((blk, LANES), 0)
    ci = _iota((blk, LANES), 1)
    lo = ci < HEAD_DIM
    incl = ri >= ci
    strict = ri > ci
    tri_f = jnp.where(incl, 1.0, 0.0)
    blockdiag = (ri < HEAD_DIM) == lo
    eye_f = jnp.where(ri == ci, 1.0, 0.0)
    lane_row = _iota((1, LANES), 1)

    def same_block(log2_size):
        return lax.shift_right_logical(ri, log2_size) == lax.shift_right_logical(ci, log2_size)
    scale = HEAD_DIM ** -0.5

    def lane_pick(row, idx):
        return jnp.sum(jnp.where(lane_row == idx, row, 0.0), axis=1, keepdims=True)

    alog = [lane_pick(alog_ref[...], 2 * hp + j) for j in range(2)]
    dtb = [lane_pick(dt_ref[...], 2 * hp + j) for j in range(2)]
    nw = nw_ref[...]

    def hdot(a, b):
        return jnp.dot(a, b, precision=HIGHEST, preferred_element_type=F32)

    def hdot_nt(a, b):
        return lax.dot_general(a, b, NT_DIMS, precision=HIGHEST, preferred_element_type=F32)

    def body(c, state):
        r0 = pl.multiple_of(c * blk, blk)

        def conv(xp, w_ref):
            acc = None
            for s in range(GDN_CONV):
                term = xp[pl.ds(r0 + pad - (GDN_CONV - 1) + s, blk), :] * w_ref[s:s + 1, :]
                acc = term if acc is None else acc + term
            return _silu(acc)

        q = conv(qp, cq_ref)
        k = conv(kp, ck_ref)
        v = conv(vp, cv_ref)
        qn = q * lax.rsqrt(_head_sum(q * q, lo) + EPS) * scale
        kn = k * lax.rsqrt(_head_sum(k * k, lo) + EPS)
        sm = s_ref[pl.ds(r0, blk), :]

        def col(idx):
            return jnp.sum(jnp.where(ci == idx, sm, 0.0), axis=1, keepdims=True)

        g_col, beta = [], []
        for j in range(2):
            a_j = col(SMALL_A + 2 * hp + j)
            b_j = col(SMALL_B + 2 * hp + j)
            gk = -jnp.exp(alog[j]) * jax.nn.softplus(a_j + dtb[j])
            g_col.append(hdot(tri_f, jnp.broadcast_to(gk, (blk, LANES))))
            beta.append(jax.nn.sigmoid(b_j))
        g_pair = jnp.where(lo, g_col[0], g_col[1])
        eg_pair = jnp.exp(g_pair)
        beta_pair = jnp.where(lo, beta[0], beta[1])
        kb = kn * beta_pair
        vb = v * beta_pair
        kbg_sw = pltpu.roll(kb * eg_pair, HEAD_DIM, 1)
        rhs = (jnp.where(lo, vb, kbg_sw), jnp.where(lo, kbg_sw, vb))

        sol, attn = [], []
        for j in range(2):
            mask_j = lo if j == 0 else jnp.logical_not(lo)
            decay = jnp.exp(jnp.where(incl, g_col[j] - g_col[j].T, NEG))
            lower = jnp.where(strict, hdot_nt(jnp.where(mask_j, kb, 0.0), kn) * decay, 0.0)
            attn.append(jnp.where(incl, hdot_nt(jnp.where(mask_j, qn, 0.0), kn) * decay, 0.0))
            d1 = jnp.where(same_block(3), lower, 0.0)
            d2 = hdot(d1, d1)
            d4 = hdot(d2, d2)
            x = eye_f - d1
            x = x + hdot(x, d2)
            x = x + hdot(x, d4)
            for lg in range(4, 8):
                off = jnp.where(same_block(lg) & jnp.logical_not(same_block(lg - 1)), lower, 0.0)
                x = x - hdot(x, hdot(off, x))
            sol.append(hdot(x, rhs[j]))
        u = jnp.where(lo, sol[0], sol[1])
        w = pltpu.roll(jnp.where(lo, sol[1], sol[0]), HEAD_DIM, 1)

        v_new = u - hdot(w, state)
        intra0 = hdot(attn[0], v_new)
        intra1 = hdot(attn[1], v_new)
        o = hdot(qn * eg_pair, state) + jnp.where(lo, intra0, intra1)
        g_last = g_pair[blk - 1:blk, :]
        kd = kn * jnp.exp(g_last - g_pair)
        kv = lax.dot_general(kd, v_new, TN_DIMS, precision=HIGHEST, preferred_element_type=F32)
        new_state = state * jnp.exp(g_last) + jnp.where(blockdiag, kv, 0.0)

        z = z_ref[pl.ds(r0, blk), :]
        o_ref[pl.ds(r0, blk), :] = _head_rmsnorm(o, nw, lo) * _silu(z)
        return new_state

    lax.fori_loop(0, t_len // blk, body, jnp.zeros((LANES, LANES), F32))


def _gdn(proj, conv_w, a_log_row, dt_row, norm_row, bsz, t_len):
    n = proj.shape[0]
    blk = lambda cb: pl.BlockSpec((t_len, LANES), lambda b, hp: (b, cb + hp))
    cw = lambda off: pl.BlockSpec((GDN_CONV, LANES), lambda b, hp: (0, off + hp))
    row = pl.BlockSpec((1, LANES), lambda b, hp: (0, 0))
    return pl.pallas_call(
        _gdn_kernel,
        grid=(bsz, N_GDN // 2),
        in_specs=[blk(CB_GQ), blk(CB_GK), blk(CB_GV), blk(CB_GZ),
                  pl.BlockSpec((t_len, LANES), lambda b, hp: (b, CB_SMALL)),
                  cw(0), cw(3), cw(6), row, row, row],
        out_specs=pl.BlockSpec((t_len, LANES), lambda b, hp: (b, hp)),
        out_shape=jax.ShapeDtypeStruct((n, D_GDN), F32),
        scratch_shapes=[pltpu.VMEM((t_len + 8, LANES), F32)] * 3,
        compiler_params=_cparams(2, 56),
        name="gdn",
    )(proj, proj, proj, proj, proj, conv_w, conv_w, conv_w, a_log_row, dt_row, norm_row)


def _rope_table_kernel(pos_ref, inv_ref, sgn_ref, cos_ref, sin_ref):
    ang = pos_ref[...].astype(F32) * inv_ref[...]
    cos_ref[...] = jnp.cos(ang)
    sin_ref[...] = jnp.sin(ang) * sgn_ref[...]


def _rope_tables(pos_col, inv_row, sgn_row, tm=1024):
    n = pos_col.shape[0]
    row = pl.BlockSpec((1, LANES), lambda i: (0, 0))
    out = pl.BlockSpec((tm, LANES), lambda i: (i, 0))
    return pl.pallas_call(
        _rope_table_kernel,
        grid=(n // tm,),
        in_specs=[pl.BlockSpec((tm, 1), lambda i: (i, 0)), row, row],
        out_specs=[out, out],
        out_shape=[jax.ShapeDtypeStruct((n, LANES), F32)] * 2,
        compiler_params=_cparams(1, 32),
        name="rope_tables",
    )(pos_col, inv_row, sgn_row)


def _prep_kernel(q_ref, kc_ref, ks_ref, vs_ref, kw_ref, vw_ref, sq_ref, sk_ref, sv_ref, cos_ref, sin_ref,
                 qo_ref, kco_ref, kso_ref, vso_ref, kwo_ref, vwo_ref, sqo_ref, sko_ref, svo_ref):
    cos = cos_ref[...]
    sin = sin_ref[...]
    first = (_iota(cos.shape, 1) % HEAD_DIM) < (HEAD_DIM // 2)

    def rope(x):
        swapped = jnp.where(first, pltpu.roll(x, LANES - HEAD_DIM // 2, 1), pltpu.roll(x, HEAD_DIM // 2, 1))
        return x * cos + swapped * sin

    for r in range(NSA_REP):
        sl = slice(r * LANES, (r + 1) * LANES)
        qo_ref[:, sl] = rope(q_ref[:, sl]).astype(BF16)
    kco_ref[...] = rope(kc_ref[...])
    kso_ref[...] = rope(ks_ref[...]).astype(BF16)
    kwo_ref[...] = rope(kw_ref[...]).astype(BF16)
    vso_ref[...] = vs_ref[...].astype(BF16)
    vwo_ref[...] = vw_ref[...].astype(BF16)
    sqo_ref[...] = sq_ref[...].astype(BF16)
    sko_ref[...] = sk_ref[...].astype(BF16)
    svo_ref[...] = sv_ref[...].astype(BF16)


def _prep(proj, cos_t, sin_t, tm=512):
    n = proj.shape[0]
    one = lambda cb: pl.BlockSpec((tm, LANES), lambda i: (i, cb))
    two = lambda cb: pl.BlockSpec((tm, 2 * LANES), lambda i: (i, cb // 2))
    tab = pl.BlockSpec((tm, LANES), lambda i: (i, 0))
    o1 = pl.BlockSpec((tm, LANES), lambda i: (i, 0))
    o2 = pl.BlockSpec((tm, 2 * LANES), lambda i: (i, 0))
    o3 = pl.BlockSpec((tm, 3 * LANES), lambda i: (i, 0))
    s1 = lambda dt: jax.ShapeDtypeStruct((n, LANES), dt)
    s2 = jax.ShapeDtypeStruct((n, 2 * LANES), BF16)
    return pl.pallas_call(
        _prep_kernel,
        grid=(n // tm,),
        in_specs=[pl.BlockSpec((tm, 3 * LANES), lambda i: (i, CB_NQ // 3)),
                  one(CB_KC), one(CB_KS), one(CB_VS), one(CB_KW), one(CB_VW),
                  two(CB_SQ), two(CB_SK), two(CB_SV), tab, tab],
        out_specs=[o3, o1, o1, o1, o1, o1, o2, o2, o2],
        out_shape=[jax.ShapeDtypeStruct((n, 3 * LANES), BF16), s1(F32), s1(BF16), s1(BF16), s1(BF16), s1(BF16),
                   s2, s2, s2],
        compiler_params=_cparams(1, 40),
        name="prep",
    )(proj, proj, proj, proj, proj, proj, proj, proj, proj, cos_t, sin_t)


def _compress_kernel(k_ref, v_ref, pek_ref, pev_ref, w1k_ref, w1v_ref, w2k_ref, w2v_ref, ko_ref, vo_ref, xp):
    t_len = k_ref.shape[0]
    n_out = t_len // CMP_STRIDE
    lo = _iota((n_out, LANES), 1) < HEAD_DIM
    for src, pe_ref, w1_ref, w2_ref, out_ref in ((k_ref, pek_ref, w1k_ref, w2k_ref, ko_ref),
                                                 (v_ref, pev_ref, w1v_ref, w2v_ref, vo_ref)):
        xp[pl.ds(0, t_len), :] = src[...]
        xp[pl.ds(t_len, CMP_STRIDE), :] = jnp.zeros((CMP_STRIDE, LANES), F32)
        z0 = jnp.zeros((n_out, LANES), F32)
        z1 = jnp.zeros((n_out, LANES), F32)
        for l in range(CMP_LEN):
            xl = xp[pl.ds(l, n_out, stride=CMP_STRIDE), :] + pe_ref[l:l + 1, :]
            w = w1_ref[l]
            z0 = z0 + jnp.dot(jnp.where(lo, xl, 0.0).astype(BF16), w, preferred_element_type=F32)
            z1 = z1 + jnp.dot(jnp.where(lo, 0.0, xl).astype(BF16), w, preferred_element_type=F32)
        out_ref[...] = (jnp.dot(_silu(z0).astype(BF16), w2_ref[0], preferred_element_type=F32)
                        + jnp.dot(_silu(z1).astype(BF16), w2_ref[1], preferred_element_type=F32))


def _compress(kc_roped, proj, pe_k2, pe_v2, w1k, w1v, w2k, w2v, bsz, t_len):
    n_out = t_len // CMP_STRIDE
    full = lambda a: pl.BlockSpec(a.shape, lambda b: (0,) * a.ndim)
    out = pl.BlockSpec((n_out, LANES), lambda b: (b, 0))
    return pl.pallas_call(
        _compress_kernel,
        grid=(bsz,),
        in_specs=[pl.BlockSpec((t_len, LANES), lambda b: (b, 0)),
                  pl.BlockSpec((t_len, LANES), lambda b: (b, CB_VC)),
                  full(pe_k2), full(pe_v2), full(w1k), full(w1v), full(w2k), full(w2v)],
        out_specs=[out, out],
        out_shape=[jax.ShapeDtypeStruct((bsz * n_out, LANES), F32)] * 2,
        scratch_shapes=[pltpu.VMEM((t_len + CMP_STRIDE, LANES), F32)],
        compiler_params=_cparams(1, 40),
        name="compress",
    )(kc_roped, proj, pe_k2, pe_v2, w1k, w1v, w2k, w2v)


def _nsa_kernel(q_ref, s_ref, kc_ref, vc_ref, ks_ref, vs_ref, kw_ref, vw_ref, ovl_ref, nw_ref, o_ref):
    qi = pl.program_id(1)
    q0 = qi * Q_BLOCK
    qb = Q_BLOCK
    rows3 = NSA_REP * qb
    n_cmp = kc_ref.shape[0]

    ri = _iota((qb, LANES), 0)
    ci = _iota((qb, LANES), 1)
    lo = ci < HEAD_DIM
    tq = q0 + ri

    q_blocks = [q_ref[:, r * LANES:(r + 1) * LANES] * BF16(HEAD_DIM ** -0.5) for r in range(NSA_REP)]
    zero_bf = jnp.zeros((qb, LANES), BF16)
    q_grp = [jnp.concatenate([jnp.where(lo if g == 0 else jnp.logical_not(lo), qb_, zero_bf) for qb_ in q_blocks],
                             axis=0) for g in range(N_NSA_KV)]

    def tile3(m):
        return jnp.concatenate([m] * NSA_REP, axis=0)

    kc = kc_ref[...].astype(BF16)
    vc = vc_ref[...].astype(BF16)
    cmp_end = _iota((qb, n_cmp), 1) * CMP_STRIDE + (CMP_LEN - 1)
    m_c = tile3(cmp_end <= q0 + _iota((qb, n_cmp), 0))
    o_c, imp_t = [], []
    for g in range(N_NSA_KV):
        s = lax.dot_general(q_grp[g], kc, NT_DIMS, preferred_element_type=F32)
        s = jnp.where(m_c, s, NEG)
        e = jnp.where(m_c, jnp.exp(s - jnp.max(s, axis=-1, keepdims=True)), 0.0)
        p = e / jnp.maximum(jnp.sum(e, axis=-1, keepdims=True), 1e-30)
        o_c.append(jnp.dot(p.astype(BF16), vc, preferred_element_type=F32))
        p_sum = p[0:qb] + p[qb:2 * qb] + p[2 * qb:3 * qb]
        imp_t.append(jnp.dot(ovl_ref[...], p_sum.T, precision=HIGHEST, preferred_element_type=F32))

    blk = ri
    tq_lane = q0 + ci
    cur = lax.shift_right_logical(tq_lane, 6)
    forced = (blk == 0) | (blk == cur) | (blk == cur - 1)
    visible = blk * SEL_BLOCK <= tq_lane
    n_sel = SEL_BLOCK
    sel = []
    for g in range(N_NSA_KV):
        score = jnp.where(forced, FORCE_SCORE, jnp.where(visible, imp_t[g], -1.0))
        score = jnp.where(blk < n_sel, score, -2.0)
        cnt = jnp.zeros((qb, LANES), F32)
        for i in range(n_sel):
            si = score[i:i + 1, :]
            beats = (si > score) | ((si == score) & (i < blk))
            cnt = cnt + jnp.where(beats, 1.0, 0.0)
        sel_t = jnp.where(cnt < float(SEL_TOPK), 1.0, 0.0)
        sel.append(sel_t.T.astype(BF16))

    def attend(k_ref_, v_ref_, j_lo, j_hi, mask_fn):
        def body(j, carry):
            k0 = pl.multiple_of(j * LANES, LANES)
            kt = k_ref_[pl.ds(k0, LANES), :]
            vt = v_ref_[pl.ds(k0, LANES), :]
            kpos = k0 + ci
            out = []
            for g in range(N_NSA_KV):
                m_i, l_i, acc = carry[g]
                msk = tile3(mask_fn(g, j, kpos))
                s = lax.dot_general(q_grp[g], kt, NT_DIMS, preferred_element_type=F32)
                s = jnp.where(msk, s, NEG)
                m_new = jnp.maximum(m_i, jnp.max(s, axis=-1, keepdims=True))
                alpha = jnp.exp(m_i - m_new)
                p = jnp.where(msk, jnp.exp(s - m_new), 0.0)
                l_new = alpha * l_i + jnp.sum(p, axis=-1, keepdims=True)
                acc_new = alpha * acc + jnp.dot(p.astype(BF16), vt, preferred_element_type=F32)
                out.append((m_new, l_new, acc_new))
            return tuple(out)

        init = tuple((jnp.full((rows3, 1), NEG, F32), jnp.zeros((rows3, 1), F32), jnp.zeros((rows3, LANES), F32))
                     for _ in range(N_NSA_KV))
        res = lax.fori_loop(j_lo, j_hi, body, init)
        return [acc / jnp.maximum(l_i, 1e-30) for (_, l_i, acc) in res]

    def sel_mask(g, j, kpos):
        expand = jnp.where(lax.shift_right_logical(ci, 6) + 2 * j == ri, 1.0, 0.0).astype(BF16)
        picked = jnp.dot(sel[g], expand, preferred_element_type=F32) > 0.5
        return picked & (kpos <= tq)

    def win_mask(g, j, kpos):
        diff = tq - kpos
        return (diff >= 0) & (diff < WINDOW)

    o_s = attend(ks_ref, vs_ref, 0, qi + 1, sel_mask)
    o_w = attend(kw_ref, vw_ref, jnp.maximum(qi - WINDOW // LANES, 0), qi + 1, win_mask)

    gate = jax.nn.sigmoid(s_ref[...])
    nw = nw_ref[...]
    for r in range(NSA_REP):
        rs = slice(r * qb, (r + 1) * qb)
        comb = []
        for g in range(N_NSA_KV):
            h = g * NSA_REP + r
            gc, gs, gw = (gate[:, SMALL_GATE + br * N_NSA + h: SMALL_GATE + br * N_NSA + h + 1] for br in range(3))
            comb.append(gc * o_c[g][rs] + gs * o_s[g][rs] + gw * o_w[g][rs])
        o_ref[:, r * LANES:(r + 1) * LANES] = _head_rmsnorm(jnp.where(lo, comb[0], comb[1]), nw, lo)


def _nsa(q_roped, proj, kcmp, vcmp, ks, vs, kw, vw, ovl_t, norm_row, bsz, t_len):
    n = proj.shape[0]
    nq = t_len // Q_BLOCK
    n_cmp = t_len // CMP_STRIDE
    per_b = lambda rows: pl.BlockSpec((rows, LANES), lambda b, i: (b, 0))
    return pl.pallas_call(
        _nsa_kernel,
        grid=(bsz, nq),
        in_specs=[pl.BlockSpec((Q_BLOCK, 3 * LANES), lambda b, i: (b * nq + i, 0)),
                  pl.BlockSpec((Q_BLOCK, LANES), lambda b, i: (b * nq + i, CB_SMALL)),
                  per_b(n_cmp), per_b(n_cmp), per_b(t_len), per_b(t_len), per_b(t_len), per_b(t_len),
                  pl.BlockSpec(ovl_t.shape, lambda b, i: (0, 0)),
                  pl.BlockSpec((1, LANES), lambda b, i: (0, 0))],
        out_specs=pl.BlockSpec((Q_BLOCK, 3 * LANES), lambda b, i: (b * nq + i, 0)),
        out_shape=jax.ShapeDtypeStruct((n, D_NSA), F32),
        compiler_params=_cparams(2, 48),
        name="nsa",
    )(q_roped, proj, kcmp, vcmp, ks, vs, kw, vw, ovl_t, norm_row)


def _sb_kernel(q_ref, k_ref, v_ref, nw_ref, o_ref):
    qi = pl.program_id(2)
    qb = Q_BLOCK
    ri = _iota((qb, LANES), 0)
    ci = _iota((qb, LANES), 1)
    lo = ci < HEAD_DIM
    upper = jnp.where(ri > ci, 1.0, 0.0).astype(BF16)
    diag_strict = ci < ri
    q = q_ref[...] * BF16(HEAD_DIM ** -0.5)
    zero_bf = jnp.zeros((qb, LANES), BF16)

    def tile(jt, qh, carry, acc, on_diag):
        k0 = pl.multiple_of(jt * LANES, LANES)
        kt = k_ref[pl.ds(k0, LANES), :]
        vt = v_ref[pl.ds(k0, LANES), :]
        z = lax.dot_general(qh, kt, NT_DIMS, preferred_element_type=F32)
        log_1m = -(jnp.maximum(z, 0.0) + jnp.log1p(jnp.exp(-jnp.abs(z))))
        if on_diag:
            log_1m = jnp.where(diag_strict, log_1m, 0.0)
        h1 = log_1m.astype(BF16)
        r1 = log_1m - h1.astype(F32)
        h2 = r1.astype(BF16)
        h3 = (r1 - h2.astype(F32)).astype(BF16)
        later = (jnp.dot(h1, upper, preferred_element_type=F32) + jnp.dot(h2, upper, preferred_element_type=F32)
                 + jnp.dot(h3, upper, preferred_element_type=F32))
        a = jnp.exp(z + log_1m + later + carry)
        if on_diag:
            a = jnp.where(diag_strict, a, 0.0)
        acc = acc + jnp.dot(a.astype(BF16), vt, preferred_element_type=F32)
        carry = carry + jnp.sum(log_1m, axis=1, keepdims=True)
        return carry, acc

    outs = []
    for j in range(2):
        qh = jnp.where(lo if j == 0 else jnp.logical_not(lo), q, zero_bf)
        carry, acc = tile(qi, qh, jnp.zeros((qb, 1), F32), jnp.zeros((qb, LANES), F32), True)

        def cond(c):
            jt, carry_, _ = c
            return jnp.logical_and(jt >= 0, jnp.max(carry_) > EXP_UNDERFLOW)

        def body(c, qh=qh):
            jt, carry_, acc_ = c
            carry_, acc_ = tile(jt, qh, carry_, acc_, False)
            return jt - 1, carry_, acc_

        _, _, acc = lax.while_loop(cond, body, (qi - 1, carry, acc))
        outs.append(acc)
    o_ref[...] = _head_rmsnorm(jnp.where(lo, outs[0], outs[1]), nw_ref[...], lo)


def _sb(sq, sk, sv, norm_row, bsz, t_len):
    n = sq.shape[0]
    nq = t_len // Q_BLOCK
    return pl.pallas_call(
        _sb_kernel,
        grid=(bsz, N_SB // 2, nq),
        in_specs=[pl.BlockSpec((Q_BLOCK, LANES), lambda b, hp, i: (b * nq + i, hp)),
                  pl.BlockSpec((t_len, LANES), lambda b, hp, i: (b, hp)),
                  pl.BlockSpec((t_len, LANES), lambda b, hp, i: (b, hp)),
                  pl.BlockSpec((1, LANES), lambda b, hp, i: (0, 0))],
        out_specs=pl.BlockSpec((Q_BLOCK, LANES), lambda b, hp, i: (b * nq + i, hp)),
        out_shape=jax.ShapeDtypeStruct((n, D_SB), F32),
        compiler_params=_cparams(3, 32),
        name="sb",
    )(sq, sk, sv, norm_row)


def _mix_kernel(h_ref, og_ref, on_ref, os_ref, wg_ref, wn_ref, ws_ref, o_ref):
    acc = h_ref[...]
    acc = acc + jnp.dot(og_ref[...].astype(BF16), wg_ref[...], preferred_element_type=F32)
    acc = acc + jnp.dot(on_ref[...].astype(BF16), wn_ref[...], preferred_element_type=F32)
    acc = acc + jnp.dot(os_ref[...].astype(BF16), ws_ref[...], preferred_element_type=F32)
    o_ref[...] = acc


def _mix(h, o_gdn, o_nsa, o_sb, wg, wn, ws, tm=512):
    n, d = h.shape
    rows = lambda a: pl.BlockSpec((tm, a.shape[1]), lambda i: (i, 0))
    full = lambda a: pl.BlockSpec(a.shape, lambda i: (0, 0))
    return pl.pallas_call(
        _mix_kernel,
        grid=(n // tm,),
        in_specs=[rows(h), rows(o_gdn), rows(o_nsa), rows(o_sb), full(wg), full(wn), full(ws)],
        out_specs=rows(h),
        out_shape=jax.ShapeDtypeStruct((n, d), F32),
        compiler_params=_cparams(1, 40),
        name="mix",
    )(h, o_gdn, o_nsa, o_sb, wg, wn, ws)


def _ffn_kernel(h_ref, halo_ref, g_ref, wg_ref, wu_ref, cg_ref, cu_ref, wd_ref, o_ref, xn_ref, acc_ref, *, t_len):
    i = pl.program_id(0)
    j = pl.program_id(1)
    tm = h_ref.shape[0]
    pad = halo_ref.shape[0]

    @pl.when(j == 0)
    def _():
        xn_ref[pl.ds(0, pad), :] = _rmsnorm_rows(halo_ref[...], g_ref[...]).astype(BF16)
        xn_ref[pl.ds(pad, tm), :] = _rmsnorm_rows(h_ref[...], g_ref[...]).astype(BF16)
        acc_ref[...] = jnp.zeros(acc_ref.shape, F32)

    xn = xn_ref[...]
    seq_start = (i * tm) % t_len == 0
    halo_keep = jnp.where(seq_start, 0.0, 1.0)
    keep = jnp.where(_iota((tm + pad, 1), 0) < pad, halo_keep, 1.0)

    def conv(w_ref, c_ref):
        u = jnp.dot(xn, w_ref[...], preferred_element_type=F32) * keep
        out = None
        for s in range(FFN_CONV):
            term = u[pad - (FFN_CONV - 1) + s: pad - (FFN_CONV - 1) + s + tm, :] * c_ref[s:s + 1, :]
            out = term if out is None else out + term
        return out

    act = _silu(conv(wg_ref, cg_ref)) * conv(wu_ref, cu_ref)
    acc_ref[...] += jnp.dot(act.astype(BF16), wd_ref[...], preferred_element_type=F32)

    @pl.when(j == pl.num_programs(1) - 1)
    def _():
        o_ref[...] = h_ref[...] + acc_ref[...]


def _ffn(h, gain, w_up, conv_w, w_down, t_len, tm=512, tf=256):
    n, d = h.shape
    d_ff = w_down.shape[0]
    nf = d_ff // tf
    pad = 8
    return pl.pallas_call(
        functools.partial(_ffn_kernel, t_len=t_len),
        grid=(n // tm, nf),
        in_specs=[pl.BlockSpec((tm, d), lambda i, j: (i, 0)),
                  pl.BlockSpec((pad, d), lambda i, j: (jnp.maximum(i * (tm // pad) - 1, 0), 0)),
                  pl.BlockSpec((1, d), lambda i, j: (0, 0)),
                  pl.BlockSpec((d, tf), lambda i, j: (0, j)),
                  pl.BlockSpec((d, tf), lambda i, j: (0, nf + j)),
                  pl.BlockSpec((FFN_CONV, tf), lambda i, j: (0, j)),
                  pl.BlockSpec((FFN_CONV, tf), lambda i, j: (0, nf + j)),
                  pl.BlockSpec((tf, d), lambda i, j: (j, 0))],
        out_specs=pl.BlockSpec((tm, d), lambda i, j: (i, 0)),
        out_shape=jax.ShapeDtypeStruct((n, d), F32),
        scratch_shapes=[pltpu.VMEM((tm + pad, d), BF16), pltpu.VMEM((tm, d), F32)],
        compiler_params=_cparams(2, 48),
        name="ffn",
    )(h, h, gain, w_up, w_up, conv_w, conv_w, w_down)


def _ple_kernel(h_ref, p_ref, g_ref, wg_ref, wp_ref, pn_ref, fin_ref, o_ref, *, final_norm):
    h = h_ref[...]
    gate = jax.nn.sigmoid(jnp.dot(_rmsnorm_rows(h, g_ref[...]).astype(BF16), wg_ref[...], preferred_element_type=F32))
    emb = jnp.dot(p_ref[...].astype(BF16), wp_ref[...], preferred_element_type=F32)
    out = h + gate * _rmsnorm_rows(emb, pn_ref[...])
    if final_norm:
        out = _rmsnorm_rows(out, fin_ref[...])
    o_ref[...] = out


def _ple(h, p, gain, w_gate, w_ple, ple_gain, fin_gain, final_norm, tm=512):
    n, d = h.shape
    rows = lambda a: pl.BlockSpec((tm, a.shape[1]), lambda i: (i, 0))
    full = lambda a: pl.BlockSpec(a.shape, lambda i: (0, 0))
    return pl.pallas_call(
        functools.partial(_ple_kernel, final_norm=final_norm),
        grid=(n // tm,),
        in_specs=[rows(h), rows(p), full(gain), full(w_gate), full(w_ple), full(ple_gain), full(fin_gain)],
        out_specs=rows(h),
        out_shape=jax.ShapeDtypeStruct((n, d), F32),
        compiler_params=_cparams(1, 40),
        name="ple",
    )(h, p, gain, w_gate, w_ple, ple_gain, fin_gain)


def _overlap_t(t_len):
    n_cmp = t_len // CMP_STRIDE
    n_sel = t_len // SEL_BLOCK
    c0 = np.arange(n_cmp) * CMP_STRIDE
    s0 = np.arange(n_sel) * SEL_BLOCK
    ov = np.clip(np.minimum(c0[None, :] + CMP_LEN, s0[:, None] + SEL_BLOCK) - np.maximum(c0[None, :], s0[:, None]), 0, None)
    out = np.zeros((LANES, n_cmp), np.float32)
    out[:n_sel] = ov.astype(np.float32) / CMP_LEN
    return jnp.asarray(out)


def _tile_row(v, reps):
    return jnp.tile(v.astype(F32), reps).reshape(1, -1)


def kernel(x, p, positions, ln_mix, w_in, gdn_conv, gdn_a_log, gdn_dt_bias, gdn_norm, nsa_pe_k, nsa_pe_v, nsa_cmp_k_w1, nsa_cmp_k_w2, nsa_cmp_v_w1, nsa_cmp_v_w2, nsa_norm, sb_norm, w_out, ln_ffn, w_up, ffn_conv, w_down, ln_ple, w_ple_gate, w_ple, ple_norm, ln_final):
    bsz, t_len, d_model = x.shape
    depth = w_in.shape[0]
    n = bsz * t_len
    assert t_len % Q_BLOCK == 0 and t_len // SEL_BLOCK <= 64 and t_len // SEL_BLOCK > 2

    cols = _inproj_columns()
    col_idx = jnp.asarray(np.maximum(cols, 0))
    col_valid = jnp.asarray((cols >= 0).astype(np.float32))
    mix_rows = jnp.asarray(_mix_rows())
    ovl_t = _overlap_t(t_len)

    half = HEAD_DIM // 2
    inv = ROPE_THETA ** (-jnp.arange(half, dtype=F32) / half)
    inv_row = jnp.tile(inv, LANES // half).reshape(1, LANES)
    sgn_row = jnp.asarray(np.where((np.arange(LANES) % HEAD_DIM) < half, -1.0, 1.0).astype(np.float32)).reshape(1, LANES)
    cos_t, sin_t = _rope_tables(positions.reshape(n, 1), inv_row, sgn_row)

    pad_lanes = lambda v: jnp.pad(v.astype(F32), (0, LANES - v.shape[0])).reshape(1, LANES)
    h = x.reshape(n, d_model)
    for i in range(depth):
        w_in_p = (jnp.take(w_in[i], col_idx, axis=1) * col_valid).astype(BF16)
        proj = _inproj(h, ln_mix[i].reshape(1, d_model), w_in_p)

        o_gdn = _gdn(proj, gdn_conv[i], pad_lanes(gdn_a_log[i]), pad_lanes(gdn_dt_bias[i]),
                     _tile_row(gdn_norm[i], 2), bsz, t_len)

        q_r, kc_r, ks_r, vs_b, kw_r, vw_b, sq_b, sk_b, sv_b = _prep(proj, cos_t, sin_t)
        w1dup = lambda w1: jnp.tile(w1.reshape(CMP_LEN, HEAD_DIM, -1), (1, 2, 1)).astype(BF16)
        w2pad = lambda w2: jnp.stack([jnp.pad(w2, ((0, 0), (0, HEAD_DIM))), jnp.pad(w2, ((0, 0), (HEAD_DIM, 0)))]).astype(BF16)
        kcmp, vcmp = _compress(kc_r, proj, jnp.tile(nsa_pe_k[i], (1, 2)), jnp.tile(nsa_pe_v[i], (1, 2)),
                               w1dup(nsa_cmp_k_w1[i]), w1dup(nsa_cmp_v_w1[i]),
                               w2pad(nsa_cmp_k_w2[i]), w2pad(nsa_cmp_v_w2[i]), bsz, t_len)
        o_nsa = _nsa(q_r, proj, kcmp, vcmp, ks_r, vs_b, kw_r, vw_b, ovl_t, _tile_row(nsa_norm[i], 2), bsz, t_len)
        o_sb = _sb(sq_b, sk_b, sv_b, _tile_row(sb_norm[i], 2), bsz, t_len)

        w_o = jnp.take(w_out[i], mix_rows, axis=0).astype(BF16)
        h = _mix(h, o_gdn, o_nsa, o_sb, w_o[:D_GDN], w_o[D_GDN:D_GDN + D_NSA], w_o[D_GDN + D_NSA:])

        h = _ffn(h, ln_ffn[i].reshape(1, d_model), w_up[i].astype(BF16), ffn_conv[i], w_down[i].astype(BF16), t_len)

        h = _ple(h, p[i].reshape(n, -1), ln_ple[i].reshape(1, d_model), w_ple_gate[i].astype(BF16),
                 w_ple[i].astype(BF16), ple_norm[i].reshape(1, d_model), ln_final.reshape(1, d_model),
                 final_norm=(i == depth - 1))
    return h.reshape(bsz, t_len, d_model)
```

```python
import functools

import jax
import jax.numpy as jnp
import numpy as np
from jax import lax
from jax.experimental import pallas as pl
from jax.experimental.pallas import tpu as pltpu

F32 = jnp.float32
BF16 = jnp.bfloat16
HIGHEST = lax.Precision.HIGHEST

LANES = 128
HEAD_DIM = 64
N_GDN = 6
N_NSA = 6
N_NSA_KV = 2
NSA_REP = N_NSA // N_NSA_KV
N_SB = 4
D_GDN = N_GDN * HEAD_DIM
D_NSA = N_NSA * HEAD_DIM
D_SB = N_SB * HEAD_DIM
GDN_CONV = 4
GDN_BLOCK = 128
CMP_LEN = 32
CMP_STRIDE = 16
SEL_BLOCK = 64
SEL_TOPK = 16
WINDOW = 512
Q_BLOCK = 128
NSA_TILES = 4
FFN_CONV = 3
ROPE_THETA = 10000.0
EPS = 1e-6
NEG = -1e30
FORCE_SCORE = 1e4
EXP_UNDERFLOW = -104.0

CB_GQ, CB_GK, CB_GV, CB_GZ = 0, 3, 6, 9
CB_NQ = 12
CB_SMALL = 15
CB_KC, CB_VC, CB_KS, CB_VS, CB_KW, CB_VW = 16, 17, 18, 19, 20, 21
CB_SQ, CB_SK, CB_SV = 22, 24, 26
N_CB = 28
NP_IN = N_CB * LANES
SMALL_A, SMALL_B, SMALL_GATE = 0, 6, 12

NN_DIMS = (((1,), (0,)), ((), ()))
NT_DIMS = (((1,), (1,)), ((), ()))
TN_DIMS = (((0,), (0,)), ((), ()))


def _cparams(n_axes, vmem_mb):
    return pltpu.CompilerParams(dimension_semantics=("arbitrary",) * n_axes,
                                vmem_limit_bytes=vmem_mb * 1024 * 1024)


def _inproj_columns():
    d_gdn, d_nsa, d_kv, d_sb = D_GDN, D_NSA, N_NSA_KV * HEAD_DIM, D_SB
    o = {}
    off = 0
    for name, size in (("gq", d_gdn), ("gk", d_gdn), ("gv", d_gdn), ("gz", d_gdn), ("ga", N_GDN), ("gb", N_GDN),
                       ("nq", d_nsa), ("kc", d_kv), ("vc", d_kv), ("ks", d_kv), ("vs", d_kv), ("kw", d_kv),
                       ("vw", d_kv), ("ng", 3 * N_NSA), ("sq", d_sb), ("sk", d_sb), ("sv", d_sb)):
        o[name] = off
        off += size
    cols = -np.ones((NP_IN,), np.int64)

    def put(cb, lane, src, size):
        cols[cb * LANES + lane: cb * LANES + lane + size] = np.arange(src, src + size)

    put(CB_GQ, 0, o["gq"], d_gdn)
    put(CB_GK, 0, o["gk"], d_gdn)
    put(CB_GV, 0, o["gv"], d_gdn)
    put(CB_GZ, 0, o["gz"], d_gdn)
    for r in range(NSA_REP):
        for g in range(N_NSA_KV):
            put(CB_NQ + r, g * HEAD_DIM, o["nq"] + (g * NSA_REP + r) * HEAD_DIM, HEAD_DIM)
    put(CB_SMALL, SMALL_A, o["ga"], N_GDN)
    put(CB_SMALL, SMALL_B, o["gb"], N_GDN)
    put(CB_SMALL, SMALL_GATE, o["ng"], 3 * N_NSA)
    for cb, nm in ((CB_KC, "kc"), (CB_VC, "vc"), (CB_KS, "ks"), (CB_VS, "vs"), (CB_KW, "kw"), (CB_VW, "vw")):
        put(cb, 0, o[nm], d_kv)
    put(CB_SQ, 0, o["sq"], d_sb)
    put(CB_SK, 0, o["sk"], d_sb)
    put(CB_SV, 0, o["sv"], d_sb)
    return cols


def _mix_rows():
    rows = list(range(D_GDN))
    for c in range(D_NSA):
        r, half, d = c // LANES, (c % LANES) // HEAD_DIM, c % HEAD_DIM
        rows.append(D_GDN + (half * NSA_REP + r) * HEAD_DIM + d)
    rows += list(range(D_GDN + D_NSA, D_GDN + D_NSA + D_SB))
    return np.asarray(rows, np.int64)


def _dot_bf16(a, b, dims=NN_DIMS):
    return lax.dot_general(a.astype(BF16), b.astype(BF16), dims, preferred_element_type=F32)


def _dot_bf16x3(a, b, dims=NN_DIMS):
    ah = a.astype(BF16)
    al = (a - ah.astype(F32)).astype(BF16)
    bh = b.astype(BF16)
    bl = (b - bh.astype(F32)).astype(BF16)
    d = lambda x, y: lax.dot_general(x, y, dims, preferred_element_type=F32)
    return (d(ah, bl) + d(al, bh)) + d(ah, bh)


def _dot_exact01(m01_bf16, x):
    h1 = x.astype(BF16)
    r1 = x - h1.astype(F32)
    h2 = r1.astype(BF16)
    h3 = (r1 - h2.astype(F32)).astype(BF16)
    d = lambda y: jnp.dot(m01_bf16, y, preferred_element_type=F32)
    return (d(h3) + d(h2)) + d(h1)


def _dot_exact01_rhs(x, m01_bf16):
    h1 = x.astype(BF16)
    r1 = x - h1.astype(F32)
    h2 = r1.astype(BF16)
    h3 = (r1 - h2.astype(F32)).astype(BF16)
    d = lambda y: jnp.dot(y, m01_bf16, preferred_element_type=F32)
    return (d(h3) + d(h2)) + d(h1)


def _permute_static(w, index, axis):
    index = np.asarray(index)
    pieces, start = [], 0
    while start < len(index):
        stop = start + 1
        if index[start] < 0:
            while stop < len(index) and index[stop] < 0:
                stop += 1
            shape = list(w.shape)
            shape[axis] = stop - start
            pieces.append(jnp.zeros(shape, w.dtype))
        else:
            while stop < len(index) and index[stop] == index[stop - 1] + 1:
                stop += 1
            pieces.append(lax.slice_in_dim(w, int(index[start]), int(index[stop - 1]) + 1, axis=axis))
        start = stop
    return jnp.concatenate(pieces, axis=axis)


def _iota(shape, dim):
    return lax.broadcasted_iota(jnp.int32, shape, dim)


def _silu(x):
    return x * jax.nn.sigmoid(x)


def _head_sum(x, lo):
    s0 = jnp.sum(jnp.where(lo, x, 0.0), axis=1, keepdims=True)
    s1 = jnp.sum(jnp.where(lo, 0.0, x), axis=1, keepdims=True)
    return jnp.where(lo, s0, s1)


def _head_rmsnorm(x, w_row, lo):
    return x * lax.rsqrt(_head_sum(x * x, lo) * (1.0 / HEAD_DIM) + EPS) * w_row


def _rmsnorm_rows(x, w_row):
    return x * lax.rsqrt(jnp.mean(x * x, axis=-1, keepdims=True) + EPS) * w_row


def _inproj_kernel(x_ref, g_ref, w_ref, o_ref, xn_ref):
    @pl.when(pl.program_id(1) == 0)
    def _():
        xn_ref[...] = _rmsnorm_rows(x_ref[...], g_ref[...]).astype(BF16)

    o_ref[...] = jnp.dot(xn_ref[...], w_ref[...], preferred_element_type=F32)


def _inproj(h, gain, w_bf16, tm=512, tn=1792):
    n, d = h.shape
    return pl.pallas_call(
        _inproj_kernel,
        grid=(n // tm, NP_IN // tn),
        in_specs=[pl.BlockSpec((tm, d), lambda i, j: (i, 0)),
                  pl.BlockSpec((1, d), lambda i, j: (0, 0)),
                  pl.BlockSpec((d, tn), lambda i, j: (0, j))],
        out_specs=pl.BlockSpec((tm, tn), lambda i, j: (i, j)),
        out_shape=jax.ShapeDtypeStruct((n, NP_IN), F32),
        scratch_shapes=[pltpu.VMEM((tm, d), BF16)],
        compiler_params=_cparams(2, 40),
        name="inproj",
    )(h, gain, w_bf16)


def _gdn_kernel(q_ref, k_ref, v_ref, z_ref, s_ref, qh_ref, kh_ref, vh_ref, cq_ref, ck_ref, cv_ref, alog_ref, dt_ref,
                nw_ref, o_ref, qp, kp, vp, st_ref):
    ti = pl.program_id(1)
    t_blk = q_ref.shape[0]
    blk = GDN_BLOCK
    pad = qh_ref.shape[0]
    n_pairs = N_GDN // 2
    halo_keep = jnp.where(ti == 0, 0.0, 1.0)
    for src, halo, dst in ((q_ref, qh_ref, qp), (k_ref, kh_ref, kp), (v_ref, vh_ref, vp)):
        for hp in range(n_pairs):
            dst[hp, pl.ds(0, pad), :] = halo[:, hp * LANES:(hp + 1) * LANES] * halo_keep
            dst[hp, pl.ds(pad, t_blk), :] = src[:, hp * LANES:(hp + 1) * LANES]

    @pl.when(ti == 0)
    def _():
        st_ref[...] = jnp.zeros(st_ref.shape, F32)

    ri = _iota((blk, LANES), 0)
    ci = _iota((blk, LANES), 1)
    lo = ci < HEAD_DIM
    incl = ri >= ci
    strict = ri > ci
    tri_f = jnp.where(incl, 1.0, 0.0)
    blockdiag = (ri < HEAD_DIM) == lo
    eye_f = jnp.where(ri == ci, 1.0, 0.0)

    def same_block(log2_size):
        return lax.shift_right_logical(ri, log2_size) == lax.shift_right_logical(ci, log2_size)
    scale = HEAD_DIM ** -0.5
    nw = nw_ref[...]

    tri_bf = tri_f.astype(BF16)
    idot = _dot_bf16x3
    rdot = _dot_bf16

    pairs = range(n_pairs)
    heads = [(hp, j) for hp in pairs for j in range(2)]

    def body(c, states):
        r0 = pl.multiple_of(c * blk, blk)

        def conv(xp, w_ref, hp):
            acc = None
            for s in range(GDN_CONV):
                term = (xp[hp, pl.ds(r0 + pad - (GDN_CONV - 1) + s, blk), :]
                        * w_ref[s:s + 1, hp * LANES:(hp + 1) * LANES])
                acc = term if acc is None else acc + term
            return _silu(acc)

        sm = s_ref[pl.ds(r0, blk), :]
        gk, beta = [], []
        for hp, j in heads:
            h = 2 * hp + j
            a_h = sm[:, SMALL_A + h:SMALL_A + h + 1]
            b_h = sm[:, SMALL_B + h:SMALL_B + h + 1]
            gk.append(jnp.broadcast_to(-jnp.exp(alog_ref[:, h:h + 1]) * jax.nn.softplus(a_h + dt_ref[:, h:h + 1]),
                                       (blk, LANES)))
            beta.append(jax.nn.sigmoid(b_h))
        g_col = [_dot_exact01(tri_bf, g) for g in gk]

        qn, kn, kb, vb, g_pair, eg_pair, rhs = [], [], [], [], [], [], []
        for hp in pairs:
            q = conv(qp, cq_ref, hp)
            k = conv(kp, ck_ref, hp)
            v = conv(vp, cv_ref, hp)
            qn.append(q * lax.rsqrt(_head_sum(q * q, lo) + EPS) * scale)
            kn.append(k * lax.rsqrt(_head_sum(k * k, lo) + EPS))
            g_pair.append(jnp.where(lo, g_col[2 * hp], g_col[2 * hp + 1]))
            eg_pair.append(jnp.exp(g_pair[hp]))
            beta_pair = jnp.where(lo, beta[2 * hp], beta[2 * hp + 1])
            kb.append(kn[hp] * beta_pair)
            vb.append(v * beta_pair)
            kbg_sw = pltpu.roll(kb[hp] * eg_pair[hp], HEAD_DIM, 1)
            rhs += [jnp.where(lo, vb[hp], kbg_sw), jnp.where(lo, kbg_sw, vb[hp])]

        mask = [lo if j == 0 else jnp.logical_not(lo) for _, j in heads]
        decay = [jnp.exp(jnp.where(incl, g - g.T, NEG)) for g in g_col]
        kk = [rdot(jnp.where(mask[i], kb[hp], 0.0), kn[hp], NT_DIMS) for i, (hp, _) in enumerate(heads)]
        qk = [rdot(jnp.where(mask[i], qn[hp], 0.0), kn[hp], NT_DIMS) for i, (hp, _) in enumerate(heads)]
        lower = [jnp.where(strict, a * d, 0.0) for a, d in zip(kk, decay)]
        attn = [jnp.where(incl, a * d, 0.0) for a, d in zip(qk, decay)]
        d1 = [jnp.where(same_block(3), a, 0.0) for a in lower]
        d2 = [idot(a, a) for a in d1]
        x = [eye_f - a for a in d1]
        x = [xi + idot(xi, a) for xi, a in zip(x, d2)]
        d4 = [idot(a, a) for a in d2]
        x = [xi + idot(xi, a) for xi, a in zip(x, d4)]
        for lg in range(4, 8):
            off_mask = same_block(lg) & jnp.logical_not(same_block(lg - 1))
            y = [idot(jnp.where(off_mask, a, 0.0), xi) for a, xi in zip(lower, x)]
            x = [xi - idot(xi, yi) for xi, yi in zip(x, y)]
        sol = [idot(xi, r) for xi, r in zip(x, rhs)]

        u = [jnp.where(lo, sol[2 * hp], sol[2 * hp + 1]) for hp in pairs]
        w = [pltpu.roll(jnp.where(lo, sol[2 * hp + 1], sol[2 * hp]), HEAD_DIM, 1) for hp in pairs]
        v_new = [u[hp] - rdot(w[hp], states[hp]) for hp in pairs]
        inter = [rdot(qn[hp] * eg_pair[hp], states[hp]) for hp in pairs]
        intra = [rdot(attn[i], v_new[hp]) for i, (hp, _) in enumerate(heads)]
        g_last = [g[blk - 1:blk, :] for g in g_pair]
        kv = [rdot(kn[hp] * jnp.exp(g_last[hp] - g_pair[hp]), v_new[hp], TN_DIMS) for hp in pairs]
        new_states = []
        for hp in pairs:
            o = inter[hp] + jnp.where(lo, intra[2 * hp], intra[2 * hp + 1])
            cs = slice(hp * LANES, (hp + 1) * LANES)
            o_ref[pl.ds(r0, blk), cs] = _head_rmsnorm(o, nw, lo) * _silu(z_ref[pl.ds(r0, blk), cs])
            new_states.append(states[hp] * jnp.exp(g_last[hp]) + jnp.where(blockdiag, kv[hp], 0.0))
        return tuple(new_states)

    states = lax.fori_loop(0, t_blk // blk, body, tuple(st_ref[hp] for hp in pairs))
    for hp in pairs:
        st_ref[hp] = states[hp]


def _gdn(proj, conv_w, a_log_row, dt_row, norm_row, bsz, t_len, t_blk=1024):
    n = proj.shape[0]
    nt = t_len // t_blk
    pad = 8
    w3 = 3 * LANES
    main = lambda cb: pl.BlockSpec((t_blk, w3), lambda b, t: (b * nt + t, cb // 3))
    halo = lambda cb: pl.BlockSpec((pad, w3), lambda b, t: (jnp.maximum((b * nt + t) * (t_blk // pad) - 1, 0), cb // 3))
    cw = lambda j: pl.BlockSpec((GDN_CONV, w3), lambda b, t: (0, j))
    row = pl.BlockSpec((1, LANES), lambda b, t: (0, 0))
    return pl.pallas_call(
        _gdn_kernel,
        grid=(bsz, nt),
        in_specs=[main(CB_GQ), main(CB_GK), main(CB_GV), main(CB_GZ),
                  pl.BlockSpec((t_blk, LANES), lambda b, t: (b * nt + t, CB_SMALL)),
                  halo(CB_GQ), halo(CB_GK), halo(CB_GV),
                  cw(0), cw(1), cw(2), row, row, row],
        out_specs=pl.BlockSpec((t_blk, w3), lambda b, t: (b * nt + t, 0)),
        out_shape=jax.ShapeDtypeStruct((n, D_GDN), F32),
        scratch_shapes=[pltpu.VMEM((N_GDN // 2, t_blk + pad, LANES), F32)] * 3 +[pltpu.VMEM((N_GDN // 2, LANES, LANES), F32)],
        compiler_params=_cparams(2, 48),
        name="gdn",
    )(proj, proj, proj, proj, proj, proj, proj, proj, conv_w, conv_w, conv_w, a_log_row, dt_row, norm_row)


def _rope_table_kernel(pos_ref, inv_ref, sgn_ref, cos_ref, sin_ref):
    ang = pos_ref[...].astype(F32) * inv_ref[...]
    cos_ref[...] = jnp.cos(ang)
    sin_ref[...] = jnp.sin(ang) * sgn_ref[...]


def _rope_tables(pos_col, inv_row, sgn_row, tm=1024):
    n = pos_col.shape[0]
    row = pl.BlockSpec((1, LANES), lambda i: (0, 0))
    out = pl.BlockSpec((tm, LANES), lambda i: (i, 0))
    return pl.pallas_call(
        _rope_table_kernel,
        grid=(n // tm,),
        in_specs=[pl.BlockSpec((tm, 1), lambda i: (i, 0)), row, row],
        out_specs=[out, out],
        out_shape=[jax.ShapeDtypeStruct((n, LANES), F32)] * 2,
        compiler_params=_cparams(1, 32),
        name="rope_tables",
    )(pos_col, inv_row, sgn_row)


def _prep_kernel(q_ref, kc_ref, ks_ref, vs_ref, kw_ref, vw_ref, sq_ref, sk_ref, sv_ref, cos_ref, sin_ref,
                 qo_ref, kco_ref, kso_ref, kwo_ref, vst0_ref, vst1_ref, vwt0_ref, vwt1_ref, sqo_ref, sko_ref, svo_ref):
    cos = cos_ref[...]
    sin = sin_ref[...]
    first = (_iota(cos.shape, 1) % HEAD_DIM) < (HEAD_DIM // 2)

    def rope(x):
        swapped = jnp.where(first, pltpu.roll(x, LANES - HEAD_DIM // 2, 1), pltpu.roll(x, HEAD_DIM // 2, 1))
        return x * cos + swapped * sin

    for r in range(NSA_REP):
        sl = slice(r * LANES, (r + 1) * LANES)
        qo_ref[:, sl] = rope(q_ref[:, sl]).astype(BF16)
    kco_ref[...] = rope(kc_ref[...])
    kso_ref[...] = rope(ks_ref[...]).astype(BF16)
    kwo_ref[...] = rope(kw_ref[...]).astype(BF16)
    top = _iota((LANES, LANES), 0) < HEAD_DIM
    for src, dst0, dst1 in ((vs_ref, vst0_ref, vst1_ref), (vw_ref, vwt0_ref, vwt1_ref)):
        for a in range(src.shape[0] // LANES):
            vt = src[a * LANES:(a + 1) * LANES, :].T
            dst0[a] = jnp.where(top, vt, 1.0).astype(BF16)
            dst1[a] = jnp.where(top, 1.0, vt).astype(BF16)
    sqo_ref[...] = sq_ref[...].astype(BF16)
    sko_ref[...] = sk_ref[...].astype(BF16)
    svo_ref[...] = sv_ref[...].astype(BF16)


def _prep(proj, cos_t, sin_t, tm=512):
    n = proj.shape[0]
    one = lambda cb: pl.BlockSpec((tm, LANES), lambda i: (i, cb))
    two = lambda cb: pl.BlockSpec((tm, 2 * LANES), lambda i: (i, cb // 2))
    tab = pl.BlockSpec((tm, LANES), lambda i: (i, 0))
    o1 = pl.BlockSpec((tm, LANES), lambda i: (i, 0))
    o2 = pl.BlockSpec((tm, 2 * LANES), lambda i: (i, 0))
    o3 = pl.BlockSpec((tm, 3 * LANES), lambda i: (i, 0))
    s1 = lambda dt: jax.ShapeDtypeStruct((n, LANES), dt)
    s2 = jax.ShapeDtypeStruct((n, 2 * LANES), BF16)
    ot = pl.BlockSpec((tm // LANES, LANES, LANES), lambda i: (i, 0, 0))
    st = jax.ShapeDtypeStruct((n // LANES, LANES, LANES), BF16)
    return pl.pallas_call(
        _prep_kernel,
        grid=(n // tm,),
        in_specs=[pl.BlockSpec((tm, 3 * LANES), lambda i: (i, CB_NQ // 3)),
                  one(CB_KC), one(CB_KS), one(CB_VS), one(CB_KW), one(CB_VW),
                  two(CB_SQ), two(CB_SK), two(CB_SV), tab, tab],
        out_specs=[o3, o1, o1, o1, ot, ot, ot, ot, o2, o2, o2],
        out_shape=[jax.ShapeDtypeStruct((n, 3 * LANES), BF16), s1(F32), s1(BF16), s1(BF16), st, st, st, st,
                   s2, s2, s2],
        compiler_params=_cparams(1, 40),
        name="prep",
    )(proj, proj, proj, proj, proj, proj, proj, proj, proj, cos_t, sin_t)


def _compress_kernel(k_ref, v_ref, pek_ref, pev_ref, w1k_ref, w1v_ref, w2k_ref, w2v_ref, ko_ref, vo_ref, xp):
    t_len = k_ref.shape[0]
    n_out = t_len // CMP_STRIDE
    lo = _iota((n_out, LANES), 1) < HEAD_DIM
    for src, pe_ref, w1_ref, w2_ref, out_ref in ((k_ref, pek_ref, w1k_ref, w2k_ref, ko_ref),
                                                 (v_ref, pev_ref, w1v_ref, w2v_ref, vo_ref)):
        xp[pl.ds(0, t_len), :] = src[...]
        xp[pl.ds(t_len, CMP_STRIDE), :] = jnp.zeros((CMP_STRIDE, LANES), F32)
        z0 = jnp.zeros((n_out, LANES), F32)
        z1 = jnp.zeros((n_out, LANES), F32)
        for l in range(CMP_LEN):
            xl = xp[pl.ds(l, n_out, stride=CMP_STRIDE), :] + pe_ref[l:l + 1, :]
            w = w1_ref[l]
            z0 = z0 + jnp.dot(jnp.where(lo, xl, 0.0).astype(BF16), w, preferred_element_type=F32)
            z1 = z1 + jnp.dot(jnp.where(lo, 0.0, xl).astype(BF16), w, preferred_element_type=F32)
        out = (jnp.dot(_silu(z0).astype(BF16), w2_ref[0], preferred_element_type=F32)
               + jnp.dot(_silu(z1).astype(BF16), w2_ref[1], preferred_element_type=F32))
        out_ref[...] = out if out_ref is ko_ref else out.T


def _compress(kc_roped, proj, pe_k2, pe_v2, w1k, w1v, w2k, w2v, bsz, t_len):
    n_out = t_len // CMP_STRIDE
    full = lambda a: pl.BlockSpec(a.shape, lambda b: (0,) * a.ndim)
    out = pl.BlockSpec((n_out, LANES), lambda b: (b, 0))
    return pl.pallas_call(
        _compress_kernel,
        grid=(bsz,),
        in_specs=[pl.BlockSpec((t_len, LANES), lambda b: (b, 0)),
                  pl.BlockSpec((t_len, LANES), lambda b: (b, CB_VC)),
                  full(pe_k2), full(pe_v2), full(w1k), full(w1v), full(w2k), full(w2v)],
        out_specs=[out, pl.BlockSpec((LANES, n_out), lambda b: (b, 0))],
        out_shape=[jax.ShapeDtypeStruct((bsz * n_out, LANES), F32), jax.ShapeDtypeStruct((bsz * LANES, n_out), F32)],
        scratch_shapes=[pltpu.VMEM((t_len + CMP_STRIDE, LANES), F32)],
        compiler_params=_cparams(1, 40),
        name="compress",
    )(kc_roped, proj, pe_k2, pe_v2, w1k, w1v, w2k, w2v)


def _nsa_kernel(q_ref, s_ref, kc_ref, vct_ref, ks_ref, vst0_ref, vst1_ref, kw_ref, vwt0_ref, vwt1_ref, ovl_ref,
                nw_ref, o_ref, acc_ref):
    qi = pl.program_id(1)
    q0 = qi * Q_BLOCK
    qb = Q_BLOCK
    cols3 = NSA_REP * qb
    n_cmp = kc_ref.shape[0]
    n_sel = ks_ref.shape[0] // SEL_BLOCK

    ri = _iota((qb, LANES), 0)
    ci = _iota((qb, LANES), 1)
    top = ri < HEAD_DIM
    tq3 = q0 + (_iota((LANES, cols3), 1) & (qb - 1))
    krow3 = _iota((LANES, cols3), 0)

    q_t = [(q_ref[:, r * LANES:(r + 1) * LANES].astype(F32) * (HEAD_DIM ** -0.5)).T for r in range(NSA_REP)]
    q_grp = [jnp.concatenate([jnp.where(top if g == 0 else jnp.logical_not(top), t, 0.0) for t in q_t],
                             axis=1).astype(BF16) for g in range(N_NSA_KV)]

    kc = kc_ref[...].astype(BF16)
    vct = vct_ref[...].astype(BF16)
    m_c = (_iota((n_cmp, cols3), 0) * CMP_STRIDE + (CMP_LEN - 1)
           <= q0 + (_iota((n_cmp, cols3), 1) & (qb - 1)))
    o_c, imp_t = [], []
    for g in range(N_NSA_KV):
        s = jnp.dot(kc, q_grp[g], preferred_element_type=F32)
        s = jnp.where(m_c, s, NEG)
        e = jnp.where(m_c, jnp.exp(s - jnp.max(s, axis=0, keepdims=True)), 0.0)
        p = e / jnp.maximum(jnp.sum(e, axis=0, keepdims=True), 1e-30)
        o_c.append(jnp.dot(vct, p.astype(BF16), preferred_element_type=F32))
        p_sum = p[:, 0:qb] + p[:, qb:2 * qb] + p[:, 2 * qb:3 * qb]
        imp_t.append(jnp.dot(ovl_ref[...], p_sum, precision=HIGHEST, preferred_element_type=F32))

    blk = ri
    tq_lane = q0 + ci
    cur = lax.shift_right_logical(tq_lane, 6)
    forced = (blk == 0) | (blk == cur) | (blk == cur - 1)
    visible = blk * SEL_BLOCK <= tq_lane
    sel = []
    for g in range(N_NSA_KV):
        score = jnp.where(forced, FORCE_SCORE, jnp.where(visible, imp_t[g], -1.0))
        score = jnp.where(blk < n_sel, score, -2.0)
        cnt = jnp.zeros((qb, LANES), F32)
        for i in range(n_sel):
            si = score[i:i + 1, :]
            beats = (si > score) | ((si == score) & (i < blk))
            cnt = cnt + jnp.where(beats, 1.0, 0.0)
        sel.append(jnp.where(cnt < float(SEL_TOPK), 1.0, 0.0).astype(BF16))

    def tiles_step(k_ref_, vt_refs, tile_ids, valids, mask_fn, m_run):
        kts, kpos = [], []
        for t, ok in zip(tile_ids, valids):
            k0 = pl.multiple_of(t * LANES, LANES)
            kts.append(k_ref_[pl.ds(k0, LANES), :])
            kpos.append((k0 if ok is None else jnp.where(ok, k0, 1 << 30)) + krow3)
        groups = range(N_NSA_KV)
        raw = [[jnp.dot(kt, q_grp[g], preferred_element_type=F32) for kt in kts] for g in groups]
        msks = [[mask_fn(g, t, kp) for t, kp in zip(tile_ids, kpos)] for g in groups]
        m_out = []
        for g in groups:
            ss = [jnp.where(mk, s, NEG) for s, mk in zip(raw[g], msks[g])]
            m_new = m_run[g]
            for s in ss:
                m_new = jnp.maximum(m_new, jnp.max(s, axis=0, keepdims=True))
            acc = acc_ref[g] * jnp.exp(m_run[g] - m_new)
            ps = [jnp.where(mk, jnp.exp(s - m_new), 0.0).astype(BF16) for s, mk in zip(ss, msks[g])]
            for t, p in zip(tile_ids, ps):
                acc = acc + jnp.dot(vt_refs[g][t], p, preferred_element_type=F32)
            acc_ref[g] = acc
            m_out.append(m_new)
        return tuple(m_out)

    def normalised():
        outs = []
        for g in range(N_NSA_KV):
            acc = acc_ref[g]
            denom = acc[HEAD_DIM:HEAD_DIM + 1, :] if g == 0 else acc[0:1, :]
            outs.append(acc / jnp.maximum(denom, 1e-30))
        return outs

    def reset():
        for g in range(N_NSA_KV):
            acc_ref[g] = jnp.zeros((LANES, cols3), F32)
        return tuple(jnp.full((1, cols3), NEG, F32) for _ in range(N_NSA_KV))

    def sel_mask(g, t, kpos3):
        expand = jnp.where(lax.shift_right_logical(ri, 6) + 2 * t == ci, 1.0, 0.0).astype(BF16)
        picked = jnp.dot(expand, sel[g], preferred_element_type=F32)
        return (jnp.concatenate([picked] * NSA_REP, axis=1) > 0.5) & (kpos3 <= tq3)

    def win_mask(g, t, kpos3):
        diff = tq3 - kpos3
        return (diff >= 0) & (diff < WINDOW)

    m0 = reset()
    lax.fori_loop(0, (qi + NSA_TILES) // NSA_TILES,
                  lambda j, m: tiles_step(ks_ref, (vst0_ref, vst1_ref), [NSA_TILES * j + a for a in range(NSA_TILES)],
                                          [None] * NSA_TILES, sel_mask, m), m0)
    o_s = normalised()
    m0 = reset()
    w_tiles = [qi - WINDOW // LANES + a for a in range(WINDOW // LANES + 1)]
    tiles_step(kw_ref, (vwt0_ref, vwt1_ref), [jnp.maximum(t, 0) for t in w_tiles], [t >= 0 for t in w_tiles],
               win_mask, m0)
    o_w = normalised()

    gate_t = jax.nn.sigmoid(s_ref[...]).T
    nw_col = nw_ref[...]
    for r in range(NSA_REP):
        cs = slice(r * qb, (r + 1) * qb)
        comb = []
        for g in range(N_NSA_KV):
            h = g * NSA_REP + r
            gc, gs, gw = (gate_t[SMALL_GATE + br * N_NSA + h: SMALL_GATE + br * N_NSA + h + 1, :] for br in range(3))
            comb.append(gc * o_c[g][:, cs] + gs * o_s[g][:, cs] + gw * o_w[g][:, cs])
        o_t = jnp.where(top, comb[0], comb[1])
        sq = o_t * o_t
        ms0 = jnp.sum(jnp.where(top, sq, 0.0), axis=0, keepdims=True)
        ms1 = jnp.sum(jnp.where(top, 0.0, sq), axis=0, keepdims=True)
        o_t = o_t * lax.rsqrt(jnp.where(top, ms0, ms1) * (1.0 / HEAD_DIM) + EPS) * nw_col
        o_ref[:, r * LANES:(r + 1) * LANES] = o_t.T


def _nsa(q_roped, proj, kcmp, vcmp_t, ks, vst0, vst1, kw, vwt0, vwt1, ovl_t, norm_col, bsz, t_len):
    n = proj.shape[0]
    nq = t_len // Q_BLOCK
    n_cmp = t_len // CMP_STRIDE
    per_b = lambda rows: pl.BlockSpec((rows, LANES), lambda b, i: (b, 0))
    tiles = pl.BlockSpec((t_len // LANES, LANES, LANES), lambda b, i: (b, 0, 0))
    return pl.pallas_call(
        _nsa_kernel,
        grid=(bsz, nq),
        in_specs=[pl.BlockSpec((Q_BLOCK, 3 * LANES), lambda b, i: (b * nq + i, 0)),
                  pl.BlockSpec((Q_BLOCK, LANES), lambda b, i: (b * nq + i, CB_SMALL)),
                  per_b(n_cmp), pl.BlockSpec((LANES, n_cmp), lambda b, i: (b, 0)),
                  per_b(t_len), tiles, tiles, per_b(t_len), tiles, tiles,
                  pl.BlockSpec(ovl_t.shape, lambda b, i: (0, 0)),
                  pl.BlockSpec((LANES, 1), lambda b, i: (0, 0))],
        out_specs=pl.BlockSpec((Q_BLOCK, 3 * LANES), lambda b, i: (b * nq + i, 0)),
        out_shape=jax.ShapeDtypeStruct((n, D_NSA), F32),
        scratch_shapes=[pltpu.VMEM((N_NSA_KV, LANES, NSA_REP * Q_BLOCK), F32)],
        compiler_params=_cparams(2, 48),
        name="nsa",
    )(q_roped, proj, kcmp, vcmp_t, ks, vst0, vst1, kw, vwt0, vwt1, ovl_t, norm_col)


def _sb_kernel(q_ref, k_ref, v_ref, nw_ref, o_ref):
    qi = pl.program_id(2)
    qb = Q_BLOCK
    ri = _iota((qb, LANES), 0)
    ci = _iota((qb, LANES), 1)
    lo = ci < HEAD_DIM
    upper_ones = jnp.concatenate([jnp.where(ri > ci, 1.0, 0.0), jnp.ones((qb, LANES), F32)], axis=1).astype(BF16)
    diag_strict = ci < ri
    q = q_ref[...] * BF16(HEAD_DIM ** -0.5)
    zero_bf = jnp.zeros((qb, LANES), BF16)
    q_heads = [jnp.where(lo if j == 0 else jnp.logical_not(lo), q, zero_bf) for j in range(2)]

    def tile_pair(jt, carries, accs, first_on_diag):
        units = []
        vts = []
        for slot in range(2):
            t = jt - slot
            ok = t >= 0
            k0 = pl.multiple_of(jnp.maximum(t, 0) * LANES, LANES)
            kt = k_ref[pl.ds(k0, LANES), :]
            vts.append(v_ref[pl.ds(k0, LANES), :])
            keep = jnp.where(ok, 1.0, 0.0)
            for j in range(2):
                units.append((j, slot, keep, lax.dot_general(q_heads[j], kt, NT_DIMS, preferred_element_type=F32)))
        log_1ms, sums = [], []
        for j, slot, keep, z in units:
            log_1m = -(jnp.maximum(z, 0.0) + jnp.log1p(jnp.exp(-jnp.abs(z)))) * keep
            if first_on_diag and slot == 0:
                log_1m = jnp.where(diag_strict, log_1m, 0.0)
            log_1ms.append(log_1m)
            sums.append(_dot_exact01_rhs(log_1m, upper_ones))
        carries, accs = list(carries), list(accs)
        weights = []
        for (j, slot, keep, z), log_1m, sm in zip(units, log_1ms, sums):
            a = jnp.exp(z + log_1m + sm[:, :LANES] + carries[j]) * keep
            if first_on_diag and slot == 0:
                a = jnp.where(diag_strict, a, 0.0)
            weights.append(a.astype(BF16))
            carries[j] = carries[j] + sm[:, LANES:]
        for (j, slot, _, _), a in zip(units, weights):
            accs[j] = accs[j] + jnp.dot(a, vts[slot], preferred_element_type=F32)
        return tuple(carries), tuple(accs)

    zeros = (jnp.zeros((qb, LANES), F32),) * 2
    carries, accs = tile_pair(qi, zeros, zeros, True)

    def cond(c):
        jt, carries_, _ = c
        return jnp.logical_and(jt >= 0, jnp.max(jnp.maximum(carries_[0], carries_[1])) > EXP_UNDERFLOW)

    def body(c):
        jt, carries_, accs_ = c
        carries_, accs_ = tile_pair(jt, carries_, accs_, False)
        return jt - 2, carries_, accs_

    _, _, outs = lax.while_loop(cond, body, (qi - 2, carries, accs))
    o_ref[...] = _head_rmsnorm(jnp.where(lo, outs[0], outs[1]), nw_ref[...], lo)


def _sb(sq, sk, sv, norm_row, bsz, t_len):
    n = sq.shape[0]
    nq = t_len // Q_BLOCK
    return pl.pallas_call(
        _sb_kernel,
        grid=(bsz, N_SB // 2, nq),
        in_specs=[pl.BlockSpec((Q_BLOCK, LANES), lambda b, hp, i: (b * nq + i, hp)),
                  pl.BlockSpec((t_len, LANES), lambda b, hp, i: (b, hp)),
                  pl.BlockSpec((t_len, LANES), lambda b, hp, i: (b, hp)),
                  pl.BlockSpec((1, LANES), lambda b, hp, i: (0, 0))],
        out_specs=pl.BlockSpec((Q_BLOCK, LANES), lambda b, hp, i: (b * nq + i, hp)),
        out_shape=jax.ShapeDtypeStruct((n, D_SB), F32),
        compiler_params=_cparams(3, 32),
        name="sb",
    )(sq, sk, sv, norm_row)


def _mix_kernel(h_ref, og_ref, on_ref, os_ref, wg_ref, wn_ref, ws_ref, o_ref):
    acc = h_ref[...]
    acc = acc + jnp.dot(og_ref[...].astype(BF16), wg_ref[...], preferred_element_type=F32)
    acc = acc + jnp.dot(on_ref[...].astype(BF16), wn_ref[...], preferred_element_type=F32)
    acc = acc + jnp.dot(os_ref[...].astype(BF16), ws_ref[...], preferred_element_type=F32)
    o_ref[...] = acc


def _mix(h, o_gdn, o_nsa, o_sb, wg, wn, ws, tm=512):
    n, d = h.shape
    rows = lambda a: pl.BlockSpec((tm, a.shape[1]), lambda i: (i, 0))
    full = lambda a: pl.BlockSpec(a.shape, lambda i: (0, 0))
    return pl.pallas_call(
        _mix_kernel,
        grid=(n // tm,),
        in_specs=[rows(h), rows(o_gdn), rows(o_nsa), rows(o_sb), full(wg), full(wn), full(ws)],
        out_specs=rows(h),
        out_shape=jax.ShapeDtypeStruct((n, d), F32),
        compiler_params=_cparams(1, 40),
        name="mix",
    )(h, o_gdn, o_nsa, o_sb, wg, wn, ws)


def _ffn_kernel(h_ref, halo_ref, g_ref, wg_ref, wu_ref, cg_ref, cu_ref, wd_ref, o_ref, xn_ref, acc_ref, *, t_len):
    i = pl.program_id(0)
    j = pl.program_id(1)
    tm = h_ref.shape[0]
    pad = halo_ref.shape[0]

    @pl.when(j == 0)
    def _():
        halo_keep = jnp.where((i * tm) % t_len == 0, 0.0, 1.0)
        xn_ref[pl.ds(0, pad), :] = (_rmsnorm_rows(halo_ref[...], g_ref[...]) * halo_keep).astype(BF16)
        xn_ref[pl.ds(pad, tm), :] = _rmsnorm_rows(h_ref[...], g_ref[...]).astype(BF16)
        acc_ref[...] = jnp.zeros(acc_ref.shape, F32)

    xn = xn_ref[...]

    def conv(w_ref, c_ref):
        u = jnp.dot(xn, w_ref[...], preferred_element_type=F32)
        out = None
        for s in range(FFN_CONV):
            term = u[pad - (FFN_CONV - 1) + s: pad - (FFN_CONV - 1) + s + tm, :] * c_ref[s:s + 1, :]
            out = term if out is None else out + term
        return out

    act = _silu(conv(wg_ref, cg_ref)) * conv(wu_ref, cu_ref)
    acc_ref[...] += jnp.dot(act.astype(BF16), wd_ref[...], preferred_element_type=F32)

    @pl.when(j == pl.num_programs(1) - 1)
    def _():
        o_ref[...] = h_ref[...] + acc_ref[...]


def _ffn(h, gain, w_up, conv_w, w_down, t_len, tm=512, tf=1408):
    n, d = h.shape
    d_ff = w_down.shape[0]
    nf = d_ff // tf
    pad = 16
    return pl.pallas_call(
        functools.partial(_ffn_kernel, t_len=t_len),
        grid=(n // tm, nf),
        in_specs=[pl.BlockSpec((tm, d), lambda i, j: (i, 0)),
                  pl.BlockSpec((pad, d), lambda i, j: (jnp.maximum(i * (tm // pad) - 1, 0), 0)),
                  pl.BlockSpec((1, d), lambda i, j: (0, 0)),
                  pl.BlockSpec((d, tf), lambda i, j: (0, j)),
                  pl.BlockSpec((d, tf), lambda i, j: (0, nf + j)),
                  pl.BlockSpec((FFN_CONV, tf), lambda i, j: (0, j)),
                  pl.BlockSpec((FFN_CONV, tf), lambda i, j: (0, nf + j)),
                  pl.BlockSpec((tf, d), lambda i, j: (j, 0))],
        out_specs=pl.BlockSpec((tm, d), lambda i, j: (i, 0)),
        out_shape=jax.ShapeDtypeStruct((n, d), F32),
        scratch_shapes=[pltpu.VMEM((tm + pad, d), BF16), pltpu.VMEM((tm, d), F32)],
        compiler_params=_cparams(2, 56),
        name="ffn",
    )(h, h, gain, w_up, w_up, conv_w, conv_w, w_down)


def _ple_kernel(h_ref, p_ref, g_ref, wg_ref, wp_ref, pn_ref, fin_ref, o_ref, *, final_norm):
    h = h_ref[...]
    gate = jax.nn.sigmoid(jnp.dot(_rmsnorm_rows(h, g_ref[...]).astype(BF16), wg_ref[...], preferred_element_type=F32))
    emb = jnp.dot(p_ref[...].astype(BF16), wp_ref[...], preferred_element_type=F32)
    out = h + gate * _rmsnorm_rows(emb, pn_ref[...])
    if final_norm:
        out = _rmsnorm_rows(out, fin_ref[...])
    o_ref[...] = out


def _ple(h, p, gain, w_gate, w_ple, ple_gain, fin_gain, final_norm, tm=512):
    n, d = h.shape
    rows = lambda a: pl.BlockSpec((tm, a.shape[1]), lambda i: (i, 0))
    full = lambda a: pl.BlockSpec(a.shape, lambda i: (0, 0))
    return pl.pallas_call(
        functools.partial(_ple_kernel, final_norm=final_norm),
        grid=(n // tm,),
        in_specs=[rows(h), rows(p), full(gain), full(w_gate), full(w_ple), full(ple_gain), full(fin_gain)],
        out_specs=rows(h),
        out_shape=jax.ShapeDtypeStruct((n, d), F32),
        compiler_params=_cparams(1, 40),
        name="ple",
    )(h, p, gain, w_gate, w_ple, ple_gain, fin_gain)


def _overlap_t(t_len):
    n_cmp = t_len // CMP_STRIDE
    n_sel = t_len // SEL_BLOCK
    c0 = np.arange(n_cmp) * CMP_STRIDE
    s0 = np.arange(n_sel) * SEL_BLOCK
    ov = np.clip(np.minimum(c0[None, :] + CMP_LEN, s0[:, None] + SEL_BLOCK) - np.maximum(c0[None, :], s0[:, None]), 0, None)
    out = np.zeros((LANES, n_cmp), np.float32)
    out[:n_sel] = ov.astype(np.float32) / CMP_LEN
    return jnp.asarray(out)


def _tile_row(v, reps):
    return jnp.tile(v.astype(F32), reps).reshape(1, -1)


def kernel(x, p, positions, ln_mix, w_in, gdn_conv, gdn_a_log, gdn_dt_bias, gdn_norm, nsa_pe_k, nsa_pe_v, nsa_cmp_k_w1, nsa_cmp_k_w2, nsa_cmp_v_w1, nsa_cmp_v_w2, nsa_norm, sb_norm, w_out, ln_ffn, w_up, ffn_conv, w_down, ln_ple, w_ple_gate, w_ple, ple_norm, ln_final):
    bsz, t_len, d_model = x.shape
    depth = w_in.shape[0]
    n = bsz * t_len
    assert t_len % (NSA_TILES * LANES) == 0 and 2 < t_len // SEL_BLOCK <= LANES

    cols = _inproj_columns()
    mix_rows = _mix_rows()
    ovl_t = _overlap_t(t_len)

    half = HEAD_DIM // 2
    inv = ROPE_THETA ** (-jnp.arange(half, dtype=F32) / half)
    inv_row = jnp.tile(inv, LANES // half).reshape(1, LANES)
    sgn_row = jnp.asarray(np.where((np.arange(LANES) % HEAD_DIM) < half, -1.0, 1.0).astype(np.float32)).reshape(1, LANES)
    cos_t, sin_t = _rope_tables(positions.reshape(n, 1), inv_row, sgn_row)

    pad_lanes = lambda v: jnp.pad(v.astype(F32), (0, LANES - v.shape[0])).reshape(1, LANES)
    h = x.reshape(n, d_model)
    for i in range(depth):
        w_in_p = _permute_static(w_in[i].astype(BF16), cols, axis=1)
        proj = _inproj(h, ln_mix[i].reshape(1, d_model), w_in_p)

        o_gdn = _gdn(proj, gdn_conv[i], pad_lanes(gdn_a_log[i]), pad_lanes(gdn_dt_bias[i]),
                     _tile_row(gdn_norm[i], 2), bsz, t_len)

        q_r, kc_r, ks_r, kw_r, vst0, vst1, vwt0, vwt1, sq_b, sk_b, sv_b = _prep(proj, cos_t, sin_t)
        w1dup = lambda w1: jnp.tile(w1.reshape(CMP_LEN, HEAD_DIM, -1), (1, 2, 1)).astype(BF16)
        w2pad = lambda w2: jnp.stack([jnp.pad(w2, ((0, 0), (0, HEAD_DIM))), jnp.pad(w2, ((0, 0), (HEAD_DIM, 0)))]).astype(BF16)
        kcmp, vcmp = _compress(kc_r, proj, jnp.tile(nsa_pe_k[i], (1, 2)), jnp.tile(nsa_pe_v[i], (1, 2)),
                               w1dup(nsa_cmp_k_w1[i]), w1dup(nsa_cmp_v_w1[i]),
                               w2pad(nsa_cmp_k_w2[i]), w2pad(nsa_cmp_v_w2[i]), bsz, t_len)
        o_nsa = _nsa(q_r, proj, kcmp, vcmp, ks_r, vst0, vst1, kw_r, vwt0, vwt1, ovl_t,
                     _tile_row(nsa_norm[i], 2).reshape(LANES, 1), bsz, t_len)
        o_sb = _sb(sq_b, sk_b, sv_b, _tile_row(sb_norm[i], 2), bsz, t_len)

        w_o = _permute_static(w_out[i].astype(BF16), mix_rows, axis=0)
        h = _mix(h, o_gdn, o_nsa, o_sb, w_o[:D_GDN], w_o[D_GDN:D_GDN + D_NSA], w_o[D_GDN + D_NSA:])

        h = _ffn(h, ln_ffn[i].reshape(1, d_model), w_up[i].astype(BF16), ffn_conv[i], w_down[i].astype(BF16), t_len)

        h = _ple(h, p[i].reshape(n, -1), ln_ple[i].reshape(1, d_model), w_ple_gate[i].astype(BF16),
                 w_ple[i].astype(BF16), ple_norm[i].reshape(1, d_model), ln_final.reshape(1, d_model),
                 final_norm=(i == depth - 1))
    return h.reshape(bsz, t_len, d_model)
```

```python
import functools

import jax
import jax.numpy as jnp
import numpy as np
from jax import lax
from jax.experimental import pallas as pl
from jax.experimental.pallas import tpu as pltpu

F32 = jnp.float32
BF16 = jnp.bfloat16
HIGHEST = lax.Precision.HIGHEST

LANES = 128
HEAD_DIM = 64
N_GDN = 6
N_NSA = 6
N_NSA_KV = 2
NSA_REP = N_NSA // N_NSA_KV
N_SB = 4
D_GDN = N_GDN * HEAD_DIM
D_NSA = N_NSA * HEAD_DIM
D_SB = N_SB * HEAD_DIM
GDN_CONV = 4
GDN_BLOCK = 128
CMP_LEN = 32
CMP_STRIDE = 16
SEL_BLOCK = 64
SEL_TOPK = 16
WINDOW = 512
Q_BLOCK = 128
NSA_TILES = 4
FFN_CONV = 3
ROPE_THETA = 10000.0
EPS = 1e-6
NEG = -1e30
FORCE_SCORE = 1e4
EXP_UNDERFLOW = -104.0

CB_GQ, CB_GK, CB_GV, CB_GZ = 0, 3, 6, 9
CB_NQ = 12
CB_SMALL = 15
CB_KC, CB_VC, CB_KS, CB_VS, CB_KW, CB_VW = 16, 17, 18, 19, 20, 21
CB_SQ, CB_SK, CB_SV = 22, 24, 26
N_CB = 28
NP_IN = N_CB * LANES
SMALL_A, SMALL_B, SMALL_GATE = 0, 6, 12

NN_DIMS = (((1,), (0,)), ((), ()))
NT_DIMS = (((1,), (1,)), ((), ()))
TN_DIMS = (((0,), (0,)), ((), ()))


def _cparams(n_axes, vmem_mb):
    return pltpu.CompilerParams(dimension_semantics=("arbitrary",) * n_axes,
                                vmem_limit_bytes=vmem_mb * 1024 * 1024)


def _inproj_columns():
    d_gdn, d_nsa, d_kv, d_sb = D_GDN, D_NSA, N_NSA_KV * HEAD_DIM, D_SB
    o = {}
    off = 0
    for name, size in (("gq", d_gdn), ("gk", d_gdn), ("gv", d_gdn), ("gz", d_gdn), ("ga", N_GDN), ("gb", N_GDN),
                       ("nq", d_nsa), ("kc", d_kv), ("vc", d_kv), ("ks", d_kv), ("vs", d_kv), ("kw", d_kv),
                       ("vw", d_kv), ("ng", 3 * N_NSA), ("sq", d_sb), ("sk", d_sb), ("sv", d_sb)):
        o[name] = off
        off += size
    cols = -np.ones((NP_IN,), np.int64)

    def put(cb, lane, src, size):
        cols[cb * LANES + lane: cb * LANES + lane + size] = np.arange(src, src + size)

    put(CB_GQ, 0, o["gq"], d_gdn)
    put(CB_GK, 0, o["gk"], d_gdn)
    put(CB_GV, 0, o["gv"], d_gdn)
    put(CB_GZ, 0, o["gz"], d_gdn)
    for r in range(NSA_REP):
        for g in range(N_NSA_KV):
            put(CB_NQ + r, g * HEAD_DIM, o["nq"] + (g * NSA_REP + r) * HEAD_DIM, HEAD_DIM)
    put(CB_SMALL, SMALL_A, o["ga"], N_GDN)
    put(CB_SMALL, SMALL_B, o["gb"], N_GDN)
    put(CB_SMALL, SMALL_GATE, o["ng"], 3 * N_NSA)
    for cb, nm in ((CB_KC, "kc"), (CB_VC, "vc"), (CB_KS, "ks"), (CB_VS, "vs"), (CB_KW, "kw"), (CB_VW, "vw")):
        put(cb, 0, o[nm], d_kv)
    put(CB_SQ, 0, o["sq"], d_sb)
    put(CB_SK, 0, o["sk"], d_sb)
    put(CB_SV, 0, o["sv"], d_sb)
    return cols


def _mix_rows():
    rows = list(range(D_GDN))
    for c in range(D_NSA):
        r, half, d = c // LANES, (c % LANES) // HEAD_DIM, c % HEAD_DIM
        rows.append(D_GDN + (half * NSA_REP + r) * HEAD_DIM + d)
    rows += list(range(D_GDN + D_NSA, D_GDN + D_NSA + D_SB))
    return np.asarray(rows, np.int64)


def _dot_bf16(a, b, dims=NN_DIMS):
    return lax.dot_general(a.astype(BF16), b.astype(BF16), dims, preferred_element_type=F32)


def _dot_exact01(m01_bf16, x):
    h1 = x.astype(BF16)
    r1 = x - h1.astype(F32)
    h2 = r1.astype(BF16)
    h3 = (r1 - h2.astype(F32)).astype(BF16)
    d = lambda y: jnp.dot(m01_bf16, y, preferred_element_type=F32)
    return (d(h3) + d(h2)) + d(h1)


def _dot_exact01_rhs(x, m01_bf16):
    h1 = x.astype(BF16)
    r1 = x - h1.astype(F32)
    h2 = r1.astype(BF16)
    h3 = (r1 - h2.astype(F32)).astype(BF16)
    d = lambda y: jnp.dot(y, m01_bf16, preferred_element_type=F32)
    return (d(h3) + d(h2)) + d(h1)


def _permute_static(w, index, axis):
    index = np.asarray(index)
    pieces, start = [], 0
    while start < len(index):
        stop = start + 1
        if index[start] < 0:
            while stop < len(index) and index[stop] < 0:
                stop += 1
            shape = list(w.shape)
            shape[axis] = stop - start
            pieces.append(jnp.zeros(shape, w.dtype))
        else:
            while stop < len(index) and index[stop] == index[stop - 1] + 1:
                stop += 1
            pieces.append(lax.slice_in_dim(w, int(index[start]), int(index[stop - 1]) + 1, axis=axis))
        start = stop
    return jnp.concatenate(pieces, axis=axis)


def _iota(shape, dim):
    return lax.broadcasted_iota(jnp.int32, shape, dim)


def _silu(x):
    return x * jax.nn.sigmoid(x)


def _head_sum(x, lo):
    s0 = jnp.sum(jnp.where(lo, x, 0.0), axis=1, keepdims=True)
    s1 = jnp.sum(jnp.where(lo, 0.0, x), axis=1, keepdims=True)
    return jnp.where(lo, s0, s1)


def _head_rmsnorm(x, w_row, lo):
    return x * lax.rsqrt(_head_sum(x * x, lo) * (1.0 / HEAD_DIM) + EPS) * w_row


def _rmsnorm_rows(x, w_row):
    return x * lax.rsqrt(jnp.mean(x * x, axis=-1, keepdims=True) + EPS) * w_row


def _inproj_kernel(x_ref, g_ref, w_ref, o_ref, xn_ref):
    @pl.when(pl.program_id(1) == 0)
    def _():
        xn_ref[...] = _rmsnorm_rows(x_ref[...], g_ref[...]).astype(BF16)

    o_ref[...] = jnp.dot(xn_ref[...], w_ref[...], preferred_element_type=F32)


def _inproj(h, gain, w_bf16, tm=512, tn=1792):
    n, d = h.shape
    return pl.pallas_call(
        _inproj_kernel,
        grid=(n // tm, NP_IN // tn),
        in_specs=[pl.BlockSpec((tm, d), lambda i, j: (i, 0)),
                  pl.BlockSpec((1, d), lambda i, j: (0, 0)),
                  pl.BlockSpec((d, tn), lambda i, j: (0, j))],
        out_specs=pl.BlockSpec((tm, tn), lambda i, j: (i, j)),
        out_shape=jax.ShapeDtypeStruct((n, NP_IN), F32),
        scratch_shapes=[pltpu.VMEM((tm, d), BF16)],
        compiler_params=_cparams(2, 40),
        name="inproj",
    )(h, gain, w_bf16)


def _gdn_kernel(q_ref, k_ref, v_ref, z_ref, s_ref, qh_ref, kh_ref, vh_ref, cq_ref, ck_ref, cv_ref, alog_ref, dt_ref,
                nw_ref, o_ref, qp, kp, vp, st_ref):
    ti = pl.program_id(1)
    t_blk = q_ref.shape[0]
    blk = GDN_BLOCK
    pad = qh_ref.shape[0]
    n_pairs = N_GDN // 2
    halo_keep = jnp.where(ti == 0, 0.0, 1.0)
    for src, halo, dst in ((q_ref, qh_ref, qp), (k_ref, kh_ref, kp), (v_ref, vh_ref, vp)):
        for hp in range(n_pairs):
            dst[hp, pl.ds(0, pad), :] = halo[:, hp * LANES:(hp + 1) * LANES] * halo_keep
            dst[hp, pl.ds(pad, t_blk), :] = src[:, hp * LANES:(hp + 1) * LANES]

    @pl.when(ti == 0)
    def _():
        st_ref[...] = jnp.zeros(st_ref.shape, F32)

    ri = _iota((blk, LANES), 0)
    ci = _iota((blk, LANES), 1)
    lo = ci < HEAD_DIM
    incl = ri >= ci
    strict = ri > ci
    tri_f = jnp.where(incl, 1.0, 0.0)
    blockdiag = (ri < HEAD_DIM) == lo
    eye_f = jnp.where(ri == ci, 1.0, 0.0)

    def same_block(log2_size):
        return lax.shift_right_logical(ri, log2_size) == lax.shift_right_logical(ci, log2_size)
    scale = HEAD_DIM ** -0.5
    nw = nw_ref[...]

    tri_bf = tri_f.astype(BF16)
    idot = _dot_bf16
    rdot = _dot_bf16

    pairs = range(n_pairs)
    heads = [(hp, j) for hp in pairs for j in range(2)]

    def body(c, states):
        r0 = pl.multiple_of(c * blk, blk)

        def conv(xp, w_ref, hp):
            acc = None
            for s in range(GDN_CONV):
                term = (xp[hp, pl.ds(r0 + pad - (GDN_CONV - 1) + s, blk), :]
                        * w_ref[s:s + 1, hp * LANES:(hp + 1) * LANES])
                acc = term if acc is None else acc + term
            return _silu(acc)

        sm = s_ref[pl.ds(r0, blk), :]
        gk, beta = [], []
        for hp, j in heads:
            h = 2 * hp + j
            a_h = sm[:, SMALL_A + h:SMALL_A + h + 1]
            b_h = sm[:, SMALL_B + h:SMALL_B + h + 1]
            gk.append(jnp.broadcast_to(-jnp.exp(alog_ref[:, h:h + 1]) * jax.nn.softplus(a_h + dt_ref[:, h:h + 1]),
                                       (blk, LANES)))
            beta.append(jax.nn.sigmoid(b_h))
        g_col = [_dot_exact01(tri_bf, g) for g in gk]

        qn, kn, kb, vb, g_pair, eg_pair, rhs = [], [], [], [], [], [], []
        for hp in pairs:
            q = conv(qp, cq_ref, hp)
            k = conv(kp, ck_ref, hp)
            v = conv(vp, cv_ref, hp)
            qn.append(q * lax.rsqrt(_head_sum(q * q, lo) + EPS) * scale)
            kn.append(k * lax.rsqrt(_head_sum(k * k, lo) + EPS))
            g_pair.append(jnp.where(lo, g_col[2 * hp], g_col[2 * hp + 1]))
            eg_pair.append(jnp.exp(g_pair[hp]))
            beta_pair = jnp.where(lo, beta[2 * hp], beta[2 * hp + 1])
            kb.append(kn[hp] * beta_pair)
            vb.append(v * beta_pair)
            kbg_sw = pltpu.roll(kb[hp] * eg_pair[hp], HEAD_DIM, 1)
            rhs += [jnp.where(lo, vb[hp], kbg_sw), jnp.where(lo, kbg_sw, vb[hp])]

        mask = [lo if j == 0 else jnp.logical_not(lo) for _, j in heads]
        decay = [jnp.exp(jnp.where(incl, g - g.T, NEG)) for g in g_col]
        kk = [rdot(jnp.where(mask[i], kb[hp], 0.0), kn[hp], NT_DIMS) for i, (hp, _) in enumerate(heads)]
        qk = [rdot(jnp.where(mask[i], qn[hp], 0.0), kn[hp], NT_DIMS) for i, (hp, _) in enumerate(heads)]
        lower = [jnp.where(strict, a * d, 0.0) for a, d in zip(kk, decay)]
        attn = [jnp.where(incl, a * d, 0.0) for a, d in zip(qk, decay)]
        d1 = [jnp.where(same_block(3), a, 0.0) for a in lower]
        d2 = [idot(a, a) for a in d1]
        x = [eye_f - a for a in d1]
        x = [xi + idot(xi, a) for xi, a in zip(x, d2)]
        d4 = [idot(a, a) for a in d2]
        x = [xi + idot(xi, a) for xi, a in zip(x, d4)]
        for lg in range(4, 8):
            off_mask = same_block(lg) & jnp.logical_not(same_block(lg - 1))
            y = [idot(jnp.where(off_mask, a, 0.0), xi) for a, xi in zip(lower, x)]
            x = [xi - idot(xi, yi) for xi, yi in zip(x, y)]
        sol = [idot(xi, r) for xi, r in zip(x, rhs)]

        u = [jnp.where(lo, sol[2 * hp], sol[2 * hp + 1]) for hp in pairs]
        w = [pltpu.roll(jnp.where(lo, sol[2 * hp + 1], sol[2 * hp]), HEAD_DIM, 1) for hp in pairs]
        v_new = [u[hp] - rdot(w[hp], states[hp]) for hp in pairs]
        inter = [rdot(qn[hp] * eg_pair[hp], states[hp]) for hp in pairs]
        intra = [rdot(attn[i], v_new[hp]) for i, (hp, _) in enumerate(heads)]
        g_last = [g[blk - 1:blk, :] for g in g_pair]
        kv = [rdot(kn[hp] * jnp.exp(g_last[hp] - g_pair[hp]), v_new[hp], TN_DIMS) for hp in pairs]
        new_states = []
        for hp in pairs:
            o = inter[hp] + jnp.where(lo, intra[2 * hp], intra[2 * hp + 1])
            cs = slice(hp * LANES, (hp + 1) * LANES)
            o_ref[pl.ds(r0, blk), cs] = _head_rmsnorm(o, nw, lo) * _silu(z_ref[pl.ds(r0, blk), cs])
            new_states.append(states[hp] * jnp.exp(g_last[hp]) + jnp.where(blockdiag, kv[hp], 0.0))
        return tuple(new_states)

    states = lax.fori_loop(0, t_blk // blk, body, tuple(st_ref[hp] for hp in pairs))
    for hp in pairs:
        st_ref[hp] = states[hp]


def _gdn(proj, conv_w, a_log_row, dt_row, norm_row, bsz, t_len, t_blk=1024):
    n = proj.shape[0]
    nt = t_len // t_blk
    pad = 8
    w3 = 3 * LANES
    main = lambda cb: pl.BlockSpec((t_blk, w3), lambda b, t: (b * nt + t, cb // 3))
    halo = lambda cb: pl.BlockSpec((pad, w3), lambda b, t: (jnp.maximum((b * nt + t) * (t_blk // pad) - 1, 0), cb // 3))
    cw = lambda j: pl.BlockSpec((GDN_CONV, w3), lambda b, t: (0, j))
    row = pl.BlockSpec((1, LANES), lambda b, t: (0, 0))
    return pl.pallas_call(
        _gdn_kernel,
        grid=(bsz, nt),
        in_specs=[main(CB_GQ), main(CB_GK), main(CB_GV), main(CB_GZ),
                  pl.BlockSpec((t_blk, LANES), lambda b, t: (b * nt + t, CB_SMALL)),
                  halo(CB_GQ), halo(CB_GK), halo(CB_GV),
                  cw(0), cw(1), cw(2), row, row, row],
        out_specs=pl.BlockSpec((t_blk, w3), lambda b, t: (b * nt + t, 0)),
        out_shape=jax.ShapeDtypeStruct((n, D_GDN), F32),
        scratch_shapes=[pltpu.VMEM((N_GDN // 2, t_blk + pad, LANES), F32)] * 3 +[pltpu.VMEM((N_GDN // 2, LANES, LANES), F32)],
        compiler_params=_cparams(2, 48),
        name="gdn",
    )(proj, proj, proj, proj, proj, proj, proj, proj, conv_w, conv_w, conv_w, a_log_row, dt_row, norm_row)


def _rope_table_kernel(pos_ref, inv_ref, sgn_ref, cos_ref, sin_ref):
    ang = pos_ref[...].astype(F32) * inv_ref[...]
    cos_ref[...] = jnp.cos(ang)
    sin_ref[...] = jnp.sin(ang) * sgn_ref[...]


def _rope_tables(pos_col, inv_row, sgn_row, tm=1024):
    n = pos_col.shape[0]
    row = pl.BlockSpec((1, LANES), lambda i: (0, 0))
    out = pl.BlockSpec((tm, LANES), lambda i: (i, 0))
    return pl.pallas_call(
        _rope_table_kernel,
        grid=(n // tm,),
        in_specs=[pl.BlockSpec((tm, 1), lambda i: (i, 0)), row, row],
        out_specs=[out, out],
        out_shape=[jax.ShapeDtypeStruct((n, LANES), F32)] * 2,
        compiler_params=_cparams(1, 32),
        name="rope_tables",
    )(pos_col, inv_row, sgn_row)


def _prep_kernel(q_ref, kc_ref, ks_ref, vs_ref, kw_ref, vw_ref, sq_ref, sk_ref, sv_ref, cos_ref, sin_ref,
                 qo_ref, kco_ref, kso_ref, kwo_ref, vst0_ref, vst1_ref, vwt0_ref, vwt1_ref, sqo_ref, sko_ref, svo_ref):
    cos = cos_ref[...]
    sin = sin_ref[...]
    first = (_iota(cos.shape, 1) % HEAD_DIM) < (HEAD_DIM // 2)

    def rope(x):
        swapped = jnp.where(first, pltpu.roll(x, LANES - HEAD_DIM // 2, 1), pltpu.roll(x, HEAD_DIM // 2, 1))
        return x * cos + swapped * sin

    for r in range(NSA_REP):
        sl = slice(r * LANES, (r + 1) * LANES)
        qo_ref[:, sl] = rope(q_ref[:, sl]).astype(BF16)
    kco_ref[...] = rope(kc_ref[...])
    kso_ref[...] = rope(ks_ref[...]).astype(BF16)
    kwo_ref[...] = rope(kw_ref[...]).astype(BF16)
    top = _iota((LANES, LANES), 0) < HEAD_DIM
    for src, dst0, dst1 in ((vs_ref, vst0_ref, vst1_ref), (vw_ref, vwt0_ref, vwt1_ref)):
        for a in range(src.shape[0] // LANES):
            vt = src[a * LANES:(a + 1) * LANES, :].T
            dst0[a] = jnp.where(top, vt, 1.0).astype(BF16)
            dst1[a] = jnp.where(top, 1.0, vt).astype(BF16)
    sqo_ref[...] = sq_ref[...].astype(BF16)
    sko_ref[...] = sk_ref[...].astype(BF16)
    svo_ref[...] = sv_ref[...].astype(BF16)


def _prep(proj, cos_t, sin_t, tm=512):
    n = proj.shape[0]
    one = lambda cb: pl.BlockSpec((tm, LANES), lambda i: (i, cb))
    two = lambda cb: pl.BlockSpec((tm, 2 * LANES), lambda i: (i, cb // 2))
    tab = pl.BlockSpec((tm, LANES), lambda i: (i, 0))
    o1 = pl.BlockSpec((tm, LANES), lambda i: (i, 0))
    o2 = pl.BlockSpec((tm, 2 * LANES), lambda i: (i, 0))
    o3 = pl.BlockSpec((tm, 3 * LANES), lambda i: (i, 0))
    s1 = lambda dt: jax.ShapeDtypeStruct((n, LANES), dt)
    s2 = jax.ShapeDtypeStruct((n, 2 * LANES), BF16)
    ot = pl.BlockSpec((tm // LANES, LANES, LANES), lambda i: (i, 0, 0))
    st = jax.ShapeDtypeStruct((n // LANES, LANES, LANES), BF16)
    return pl.pallas_call(
        _prep_kernel,
        grid=(n // tm,),
        in_specs=[pl.BlockSpec((tm, 3 * LANES), lambda i: (i, CB_NQ // 3)),
                  one(CB_KC), one(CB_KS), one(CB_VS), one(CB_KW), one(CB_VW),
                  two(CB_SQ), two(CB_SK), two(CB_SV), tab, tab],
        out_specs=[o3, o1, o1, o1, ot, ot, ot, ot, o2, o2, o2],
        out_shape=[jax.ShapeDtypeStruct((n, 3 * LANES), BF16), s1(F32), s1(BF16), s1(BF16), st, st, st, st,
                   s2, s2, s2],
        compiler_params=_cparams(1, 40),
        name="prep",
    )(proj, proj, proj, proj, proj, proj, proj, proj, proj, cos_t, sin_t)


def _compress_kernel(k_ref, v_ref, pek_ref, pev_ref, w1k_ref, w1v_ref, w2k_ref, w2v_ref, ko_ref, vo_ref, xp):
    t_len = k_ref.shape[0]
    n_out = t_len // CMP_STRIDE
    lo = _iota((n_out, LANES), 1) < HEAD_DIM
    for src, pe_ref, w1_ref, w2_ref, out_ref in ((k_ref, pek_ref, w1k_ref, w2k_ref, ko_ref),
                                                 (v_ref, pev_ref, w1v_ref, w2v_ref, vo_ref)):
        xp[pl.ds(0, t_len), :] = src[...]
        xp[pl.ds(t_len, CMP_STRIDE), :] = jnp.zeros((CMP_STRIDE, LANES), F32)
        z0 = jnp.zeros((n_out, LANES), F32)
        z1 = jnp.zeros((n_out, LANES), F32)
        for l in range(CMP_LEN):
            xl = xp[pl.ds(l, n_out, stride=CMP_STRIDE), :] + pe_ref[l:l + 1, :]
            w = w1_ref[l]
            z0 = z0 + jnp.dot(jnp.where(lo, xl, 0.0).astype(BF16), w, preferred_element_type=F32)
            z1 = z1 + jnp.dot(jnp.where(lo, 0.0, xl).astype(BF16), w, preferred_element_type=F32)
        out = (jnp.dot(_silu(z0).astype(BF16), w2_ref[0], preferred_element_type=F32)
               + jnp.dot(_silu(z1).astype(BF16), w2_ref[1], preferred_element_type=F32))
        out_ref[...] = out if out_ref is ko_ref else out.T


def _compress(kc_roped, proj, pe_k2, pe_v2, w1k, w1v, w2k, w2v, bsz, t_len):
    n_out = t_len // CMP_STRIDE
    full = lambda a: pl.BlockSpec(a.shape, lambda b: (0,) * a.ndim)
    out = pl.BlockSpec((n_out, LANES), lambda b: (b, 0))
    return pl.pallas_call(
        _compress_kernel,
        grid=(bsz,),
        in_specs=[pl.BlockSpec((t_len, LANES), lambda b: (b, 0)),
                  pl.BlockSpec((t_len, LANES), lambda b: (b, CB_VC)),
                  full(pe_k2), full(pe_v2), full(w1k), full(w1v), full(w2k), full(w2v)],
        out_specs=[out, pl.BlockSpec((LANES, n_out), lambda b: (b, 0))],
        out_shape=[jax.ShapeDtypeStruct((bsz * n_out, LANES), F32), jax.ShapeDtypeStruct((bsz * LANES, n_out), F32)],
        scratch_shapes=[pltpu.VMEM((t_len + CMP_STRIDE, LANES), F32)],
        compiler_params=_cparams(1, 40),
        name="compress",
    )(kc_roped, proj, pe_k2, pe_v2, w1k, w1v, w2k, w2v)


def _nsa_kernel(q_ref, s_ref, kc_ref, vct_ref, ks_ref, vst0_ref, vst1_ref, kw_ref, vwt0_ref, vwt1_ref, ovl_ref,
                nw_ref, o_ref, acc_ref):
    qi = pl.program_id(1)
    q0 = qi * Q_BLOCK
    qb = Q_BLOCK
    cols3 = NSA_REP * qb
    n_cmp = kc_ref.shape[0]
    n_sel = ks_ref.shape[0] // SEL_BLOCK

    ri = _iota((qb, LANES), 0)
    ci = _iota((qb, LANES), 1)
    top = ri < HEAD_DIM
    tq_lane = q0 + ci

    def per_head(fn, x):
        return jnp.concatenate([fn(x[:, r * qb:(r + 1) * qb]) for r in range(NSA_REP)], axis=1)

    q_t = [(q_ref[:, r * LANES:(r + 1) * LANES].astype(F32) * (HEAD_DIM ** -0.5)).T for r in range(NSA_REP)]
    q_grp = [jnp.concatenate([jnp.where(top if g == 0 else jnp.logical_not(top), t, 0.0) for t in q_t],
                             axis=1).astype(BF16) for g in range(N_NSA_KV)]

    kc = kc_ref[...].astype(BF16)
    vct = vct_ref[...].astype(BF16)
    m_c = (_iota((n_cmp, cols3), 0) * CMP_STRIDE + (CMP_LEN - 1)
           <= q0 + (_iota((n_cmp, cols3), 1) & (qb - 1)))
    o_c, imp_t = [], []
    for g in range(N_NSA_KV):
        s = jnp.dot(kc, q_grp[g], preferred_element_type=F32)
        s = jnp.where(m_c, s, NEG)
        e = jnp.where(m_c, jnp.exp(s - jnp.max(s, axis=0, keepdims=True)), 0.0)
        p = e / jnp.maximum(jnp.sum(e, axis=0, keepdims=True), 1e-30)
        o_c.append(jnp.dot(vct, p.astype(BF16), preferred_element_type=F32))
        p_sum = p[:, 0:qb] + p[:, qb:2 * qb] + p[:, 2 * qb:3 * qb]
        imp_t.append(_dot_exact01(ovl_ref[...].astype(BF16), p_sum))

    blk = ri
    tq_lane = q0 + ci
    cur = lax.shift_right_logical(tq_lane, 6)
    forced = (blk == 0) | (blk == cur) | (blk == cur - 1)
    visible = blk * SEL_BLOCK <= tq_lane
    sel = []
    for g in range(N_NSA_KV):
        score = jnp.where(forced, FORCE_SCORE, jnp.where(visible, imp_t[g], -1.0))
        slabs = [score[8 * v:8 * v + 8, :] for v in range(n_sel // 8)]
        cnts = [jnp.zeros((8, LANES), F32) for _ in slabs]
        row8 = _iota((8, LANES), 0)
        for i in range(n_sel):
            si = jnp.broadcast_to(score[i:i + 1, :], (8, LANES))
            for v, slab in enumerate(slabs):
                if 8 * v > i:
                    beats = si >= slab
                elif 8 * v + 7 < i:
                    beats = si > slab
                else:
                    beats = (si > slab) | ((si == slab) & (row8 > i - 8 * v))
                cnts[v] = cnts[v] + jnp.where(beats, 1.0, 0.0)
        picked_rows = [jnp.where(c < float(SEL_TOPK), 1.0, 0.0) for c in cnts]
        if n_sel < LANES:
            picked_rows.append(jnp.zeros((LANES - n_sel, LANES), F32))
        sel.append(jnp.concatenate(picked_rows, axis=0).astype(BF16))

    def tiles_step(k_ref_, vt_refs, tile_ids, valids, mask_fn, m_run):
        kts, kpos = [], []
        for t, ok in zip(tile_ids, valids):
            k0 = pl.multiple_of(t * LANES, LANES)
            kts.append(k_ref_[pl.ds(k0, LANES), :])
            kpos.append((k0 if ok is None else jnp.where(ok, k0, 1 << 30)) + ri)
        groups = range(N_NSA_KV)
        raw = [[jnp.dot(kt, q_grp[g], preferred_element_type=F32) for kt in kts] for g in groups]
        msks = [[mask_fn(g, t, kp) for t, kp in zip(tile_ids, kpos)] for g in groups]
        m_out = []
        for g in groups:
            ss = [per_head(lambda x, mk=mk: jnp.where(mk, x, NEG), s) for s, mk in zip(raw[g], msks[g])]
            m_new = m_run[g]
            for s in ss:
                m_new = jnp.maximum(m_new, jnp.max(s, axis=0, keepdims=True))
            acc = acc_ref[g] * jnp.exp(m_run[g] - m_new)
            ps = [per_head(lambda x, mk=mk: jnp.where(mk, x, 0.0), jnp.exp(s - m_new)).astype(BF16)
                  for s, mk in zip(ss, msks[g])]
            for t, p in zip(tile_ids, ps):
                acc = acc + jnp.dot(vt_refs[g][t], p, preferred_element_type=F32)
            acc_ref[g] = acc
            m_out.append(m_new)
        return tuple(m_out)

    def normalised():
        outs = []
        for g in range(N_NSA_KV):
            acc = acc_ref[g]
            denom = acc[HEAD_DIM:HEAD_DIM + 1, :] if g == 0 else acc[0:1, :]
            outs.append(acc / jnp.maximum(denom, 1e-30))
        return outs

    def reset():
        for g in range(N_NSA_KV):
            acc_ref[g] = jnp.zeros((LANES, cols3), F32)
        return tuple(jnp.full((1, cols3), NEG, F32) for _ in range(N_NSA_KV))

    def picked(g, t):
        expand = jnp.where(lax.shift_right_logical(ri, 6) + 2 * t == ci, 1.0, 0.0).astype(BF16)
        return jnp.dot(expand, sel[g], preferred_element_type=F32) > 0.5

    def sel_mask_interior(g, t, kpos):
        return picked(g, t)

    def sel_mask(g, t, kpos):
        return picked(g, t) & (kpos <= tq_lane)

    def win_mask(g, t, kpos):
        diff = tq_lane - kpos
        return (diff >= 0) & (diff < WINDOW)

    def sel_step(j, m, mask_fn):
        return tiles_step(ks_ref, (vst0_ref, vst1_ref), [NSA_TILES * j + a for a in range(NSA_TILES)],
                          [None] * NSA_TILES, mask_fn, m)

    last = qi // NSA_TILES
    m_sel = lax.fori_loop(0, last, lambda j, m: sel_step(j, m, sel_mask_interior), reset())
    sel_step(last, m_sel, sel_mask)
    o_s = normalised()
    m0 = reset()
    w_tiles = [qi - WINDOW // LANES + a for a in range(WINDOW // LANES + 1)]
    tiles_step(kw_ref, (vwt0_ref, vwt1_ref), [jnp.maximum(t, 0) for t in w_tiles], [t >= 0 for t in w_tiles],
               win_mask, m0)
    o_w = normalised()

    gate_t = jax.nn.sigmoid(s_ref[...]).T
    nw_col = nw_ref[...]
    for r in range(NSA_REP):
        cs = slice(r * qb, (r + 1) * qb)
        comb = []
        for g in range(N_NSA_KV):
            h = g * NSA_REP + r
            gc, gs, gw = (gate_t[SMALL_GATE + br * N_NSA + h: SMALL_GATE + br * N_NSA + h + 1, :] for br in range(3))
            comb.append(gc * o_c[g][:, cs] + gs * o_s[g][:, cs] + gw * o_w[g][:, cs])
        o_t = jnp.where(top, comb[0], comb[1])
        sq = o_t * o_t
        ms0 = jnp.sum(jnp.where(top, sq, 0.0), axis=0, keepdims=True)
        ms1 = jnp.sum(jnp.where(top, 0.0, sq), axis=0, keepdims=True)
        o_t = o_t * lax.rsqrt(jnp.where(top, ms0, ms1) * (1.0 / HEAD_DIM) + EPS) * nw_col
        o_ref[:, r * LANES:(r + 1) * LANES] = o_t.T


def _nsa(q_roped, proj, kcmp, vcmp_t, ks, vst0, vst1, kw, vwt0, vwt1, ovl_t, norm_col, bsz, t_len):
    n = proj.shape[0]
    nq = t_len // Q_BLOCK
    n_cmp = t_len // CMP_STRIDE
    per_b = lambda rows: pl.BlockSpec((rows, LANES), lambda b, i: (b, 0))
    tiles = pl.BlockSpec((t_len // LANES, LANES, LANES), lambda b, i: (b, 0, 0))
    return pl.pallas_call(
        _nsa_kernel,
        grid=(bsz, nq),
        in_specs=[pl.BlockSpec((Q_BLOCK, 3 * LANES), lambda b, i: (b * nq + i, 0)),
                  pl.BlockSpec((Q_BLOCK, LANES), lambda b, i: (b * nq + i, CB_SMALL)),
                  per_b(n_cmp), pl.BlockSpec((LANES, n_cmp), lambda b, i: (b, 0)),
                  per_b(t_len), tiles, tiles, per_b(t_len), tiles, tiles,
                  pl.BlockSpec(ovl_t.shape, lambda b, i: (0, 0)),
                  pl.BlockSpec((LANES, 1), lambda b, i: (0, 0))],
        out_specs=pl.BlockSpec((Q_BLOCK, 3 * LANES), lambda b, i: (b * nq + i, 0)),
        out_shape=jax.ShapeDtypeStruct((n, D_NSA), F32),
        scratch_shapes=[pltpu.VMEM((N_NSA_KV, LANES, NSA_REP * Q_BLOCK), F32)],
        compiler_params=_cparams(2, 48),
        name="nsa",
    )(q_roped, proj, kcmp, vcmp_t, ks, vst0, vst1, kw, vwt0, vwt1, ovl_t, norm_col)


def _sb_kernel(q_ref, k_ref, v_ref, nw_ref, o_ref):
    qi = pl.program_id(1)
    qb = Q_BLOCK
    n_pairs = N_SB // 2
    ri = _iota((qb, LANES), 0)
    ci = _iota((qb, LANES), 1)
    lo = ci < HEAD_DIM
    upper_ones = jnp.concatenate([jnp.where(ri > ci, 1.0, 0.0), jnp.ones((qb, LANES), F32)], axis=1).astype(BF16)
    diag_strict = ci < ri
    zero_bf = jnp.zeros((qb, LANES), BF16)
    q_heads = []
    for p in range(n_pairs):
        q = q_ref[:, p * LANES:(p + 1) * LANES] * BF16(HEAD_DIM ** -0.5)
        q_heads += [jnp.where(lo if j == 0 else jnp.logical_not(lo), q, zero_bf) for j in range(2)]
    n_heads = len(q_heads)

    def tile_pair(jt, carries, accs, first_on_diag):
        units = []
        vts = {}
        for slot in range(2):
            t = jt - slot
            ok = t >= 0
            k0 = pl.multiple_of(jnp.maximum(t, 0) * LANES, LANES)
            keep = jnp.where(ok, 1.0, 0.0)
            for p in range(n_pairs):
                kt = k_ref[pl.ds(k0, LANES), p * LANES:(p + 1) * LANES]
                vts[slot, p] = v_ref[pl.ds(k0, LANES), p * LANES:(p + 1) * LANES]
                for j in range(2):
                    h = 2 * p + j
                    units.append((h, slot, keep, lax.dot_general(q_heads[h], kt, NT_DIMS, preferred_element_type=F32)))
        log_1ms, sums = [], []
        for j, slot, keep, z in units:
            log_1m = -(jnp.maximum(z, 0.0) + jnp.log1p(jnp.exp(-jnp.abs(z)))) * keep
            if first_on_diag and slot == 0:
                log_1m = jnp.where(diag_strict, log_1m, 0.0)
            log_1ms.append(log_1m)
            sums.append(_dot_exact01_rhs(log_1m, upper_ones))
        carries, accs = list(carries), list(accs)
        weights = []
        for (j, slot, keep, z), log_1m, sm in zip(units, log_1ms, sums):
            a = jnp.exp(z + log_1m + sm[:, :LANES] + carries[j]) * keep
            if first_on_diag and slot == 0:
                a = jnp.where(diag_strict, a, 0.0)
            weights.append(a.astype(BF16))
            carries[j] = carries[j] + sm[:, LANES:]
        for (j, slot, _, _), a in zip(units, weights):
            accs[j] = accs[j] + jnp.dot(a, vts[slot, j // 2], preferred_element_type=F32)
        return tuple(carries), tuple(accs)

    zeros = (jnp.zeros((qb, LANES), F32),) * n_heads
    carries, accs = tile_pair(qi, zeros, zeros, True)

    def cond(c):
        jt, carries_, _ = c
        worst = carries_[0]
        for c_h in carries_[1:]:
            worst = jnp.maximum(worst, c_h)
        return jnp.logical_and(jt >= 0, jnp.max(worst) > EXP_UNDERFLOW)

    def body(c):
        jt, carries_, accs_ = c
        carries_, accs_ = tile_pair(jt, carries_, accs_, False)
        return jt - 2, carries_, accs_

    _, _, outs = lax.while_loop(cond, body, (qi - 2, carries, accs))
    for p in range(n_pairs):
        o_ref[:, p * LANES:(p + 1) * LANES] = _head_rmsnorm(jnp.where(lo, outs[2 * p], outs[2 * p + 1]), nw_ref[...], lo)


def _sb(sq, sk, sv, norm_row, bsz, t_len):
    n = sq.shape[0]
    nq = t_len // Q_BLOCK
    return pl.pallas_call(
        _sb_kernel,
        grid=(bsz, nq),
        in_specs=[pl.BlockSpec((Q_BLOCK, D_SB), lambda b, i: (b * nq + i, 0)),
                  pl.BlockSpec((t_len, D_SB), lambda b, i: (b, 0)),
                  pl.BlockSpec((t_len, D_SB), lambda b, i: (b, 0)),
                  pl.BlockSpec((1, LANES), lambda b, i: (0, 0))],
        out_specs=pl.BlockSpec((Q_BLOCK, D_SB), lambda b, i: (b * nq + i, 0)),
        out_shape=jax.ShapeDtypeStruct((n, D_SB), F32),
        compiler_params=_cparams(2, 32),
        name="sb",
    )(sq, sk, sv, norm_row)


def _mix_kernel(h_ref, og_ref, on_ref, os_ref, wg_ref, wn_ref, ws_ref, o_ref):
    acc = h_ref[...]
    acc = acc + jnp.dot(og_ref[...].astype(BF16), wg_ref[...], preferred_element_type=F32)
    acc = acc + jnp.dot(on_ref[...].astype(BF16), wn_ref[...], preferred_element_type=F32)
    acc = acc + jnp.dot(os_ref[...].astype(BF16), ws_ref[...], preferred_element_type=F32)
    o_ref[...] = acc


def _mix(h, o_gdn, o_nsa, o_sb, wg, wn, ws, tm=512):
    n, d = h.shape
    rows = lambda a: pl.BlockSpec((tm, a.shape[1]), lambda i: (i, 0))
    full = lambda a: pl.BlockSpec(a.shape, lambda i: (0, 0))
    return pl.pallas_call(
        _mix_kernel,
        grid=(n // tm,),
        in_specs=[rows(h), rows(o_gdn), rows(o_nsa), rows(o_sb), full(wg), full(wn), full(ws)],
        out_specs=rows(h),
        out_shape=jax.ShapeDtypeStruct((n, d), F32),
        compiler_params=_cparams(1, 40),
        name="mix",
    )(h, o_gdn, o_nsa, o_sb, wg, wn, ws)


def _ffn_kernel(h_ref, halo_ref, g_ref, wg_ref, wu_ref, cg_ref, cu_ref, wd_ref, o_ref, xn_ref, acc_ref, *, t_len):
    i = pl.program_id(0)
    j = pl.program_id(1)
    tm = h_ref.shape[0]
    pad = halo_ref.shape[0]

    @pl.when(j == 0)
    def _():
        halo_keep = jnp.where((i * tm) % t_len == 0, 0.0, 1.0)
        xn_ref[pl.ds(0, pad), :] = (_rmsnorm_rows(halo_ref[...], g_ref[...]) * halo_keep).astype(BF16)
        xn_ref[pl.ds(pad, tm), :] = _rmsnorm_rows(h_ref[...], g_ref[...]).astype(BF16)
        acc_ref[...] = jnp.zeros(acc_ref.shape, F32)

    xn = xn_ref[...]

    def conv(w_ref, c_ref):
        u = jnp.dot(xn, w_ref[...], preferred_element_type=F32)
        out = None
        for s in range(FFN_CONV):
            term = u[pad - (FFN_CONV - 1) + s: pad - (FFN_CONV - 1) + s + tm, :] * c_ref[s:s + 1, :]
            out = term if out is None else out + term
        return out

    act = _silu(conv(wg_ref, cg_ref)) * conv(wu_ref, cu_ref)
    acc_ref[...] += jnp.dot(act.astype(BF16), wd_ref[...], preferred_element_type=F32)

    @pl.when(j == pl.num_programs(1) - 1)
    def _():
        o_ref[...] = h_ref[...] + acc_ref[...]


def _ffn(h, gain, w_up, conv_w, w_down, t_len, tm=512, tf=1408):
    n, d = h.shape
    d_ff = w_down.shape[0]
    nf = d_ff // tf
    pad = 16
    return pl.pallas_call(
        functools.partial(_ffn_kernel, t_len=t_len),
        grid=(n // tm, nf),
        in_specs=[pl.BlockSpec((tm, d), lambda i, j: (i, 0)),
                  pl.BlockSpec((pad, d), lambda i, j: (jnp.maximum(i * (tm // pad) - 1, 0), 0)),
                  pl.BlockSpec((1, d), lambda i, j: (0, 0)),
                  pl.BlockSpec((d, tf), lambda i, j: (0, j)),
                  pl.BlockSpec((d, tf), lambda i, j: (0, nf + j)),
                  pl.BlockSpec((FFN_CONV, tf), lambda i, j: (0, j)),
                  pl.BlockSpec((FFN_CONV, tf), lambda i, j: (0, nf + j)),
                  pl.BlockSpec((tf, d), lambda i, j: (j, 0))],
        out_specs=pl.BlockSpec((tm, d), lambda i, j: (i, 0)),
        out_shape=jax.ShapeDtypeStruct((n, d), F32),
        scratch_shapes=[pltpu.VMEM((tm + pad, d), BF16), pltpu.VMEM((tm, d), F32)],
        compiler_params=_cparams(2, 56),
        name="ffn",
    )(h, h, gain, w_up, w_up, conv_w, conv_w, w_down)


def _ple_kernel(h_ref, p_ref, g_ref, wg_ref, wp_ref, pn_ref, fin_ref, o_ref, *, final_norm):
    h = h_ref[...]
    gate = jax.nn.sigmoid(jnp.dot(_rmsnorm_rows(h, g_ref[...]).astype(BF16), wg_ref[...], preferred_element_type=F32))
    emb = jnp.dot(p_ref[...].astype(BF16), wp_ref[...], preferred_element_type=F32)
    out = h + gate * _rmsnorm_rows(emb, pn_ref[...])
    if final_norm:
        out = _rmsnorm_rows(out, fin_ref[...])
    o_ref[...] = out


def _ple(h, p, gain, w_gate, w_ple, ple_gain, fin_gain, final_norm, tm=512):
    n, d = h.shape
    rows = lambda a: pl.BlockSpec((tm, a.shape[1]), lambda i: (i, 0))
    full = lambda a: pl.BlockSpec(a.shape, lambda i: (0, 0))
    return pl.pallas_call(
        functools.partial(_ple_kernel, final_norm=final_norm),
        grid=(n // tm,),
        in_specs=[rows(h), rows(p), full(gain), full(w_gate), full(w_ple), full(ple_gain), full(fin_gain)],
        out_specs=rows(h),
        out_shape=jax.ShapeDtypeStruct((n, d), F32),
        compiler_params=_cparams(1, 40),
        name="ple",
    )(h, p, gain, w_gate, w_ple, ple_gain, fin_gain)


def _overlap_t(t_len):
    n_cmp = t_len // CMP_STRIDE
    n_sel = t_len // SEL_BLOCK
    c0 = np.arange(n_cmp) * CMP_STRIDE
    s0 = np.arange(n_sel) * SEL_BLOCK
    ov = np.clip(np.minimum(c0[None, :] + CMP_LEN, s0[:, None] + SEL_BLOCK) - np.maximum(c0[None, :], s0[:, None]), 0, None)
    out = np.zeros((LANES, n_cmp), np.float32)
    out[:n_sel] = ov.astype(np.float32) / CMP_LEN
    return jnp.asarray(out)


def _tile_row(v, reps):
    return jnp.tile(v.astype(F32), reps).reshape(1, -1)


def kernel(x, p, positions, ln_mix, w_in, gdn_conv, gdn_a_log, gdn_dt_bias, gdn_norm, nsa_pe_k, nsa_pe_v, nsa_cmp_k_w1, nsa_cmp_k_w2, nsa_cmp_v_w1, nsa_cmp_v_w2, nsa_norm, sb_norm, w_out, ln_ffn, w_up, ffn_conv, w_down, ln_ple, w_ple_gate, w_ple, ple_norm, ln_final):
    bsz, t_len, d_model = x.shape
    depth = w_in.shape[0]
    n = bsz * t_len
    assert t_len % (NSA_TILES * LANES) == 0 and 2 < t_len // SEL_BLOCK <= LANES

    cols = _inproj_columns()
    mix_rows = _mix_rows()
    ovl_t = _overlap_t(t_len)

    half = HEAD_DIM // 2
    inv = ROPE_THETA ** (-jnp.arange(half, dtype=F32) / half)
    inv_row = jnp.tile(inv, LANES // half).reshape(1, LANES)
    sgn_row = jnp.asarray(np.where((np.arange(LANES) % HEAD_DIM) < half, -1.0, 1.0).astype(np.float32)).reshape(1, LANES)
    cos_t, sin_t = _rope_tables(positions.reshape(n, 1), inv_row, sgn_row)

    pad_lanes = lambda v: jnp.pad(v.astype(F32), (0, LANES - v.shape[0])).reshape(1, LANES)
    h = x.reshape(n, d_model)
    for i in range(depth):
        w_in_p = _permute_static(w_in[i].astype(BF16), cols, axis=1)
        proj = _inproj(h, ln_mix[i].reshape(1, d_model), w_in_p)

        o_gdn = _gdn(proj, gdn_conv[i], pad_lanes(gdn_a_log[i]), pad_lanes(gdn_dt_bias[i]),
                     _tile_row(gdn_norm[i], 2), bsz, t_len)

        q_r, kc_r, ks_r, kw_r, vst0, vst1, vwt0, vwt1, sq_b, sk_b, sv_b = _prep(proj, cos_t, sin_t)
        w1dup = lambda w1: jnp.tile(w1.reshape(CMP_LEN, HEAD_DIM, -1), (1, 2, 1)).astype(BF16)
        w2pad = lambda w2: jnp.stack([jnp.pad(w2, ((0, 0), (0, HEAD_DIM))), jnp.pad(w2, ((0, 0), (HEAD_DIM, 0)))]).astype(BF16)
        kcmp, vcmp = _compress(kc_r, proj, jnp.tile(nsa_pe_k[i], (1, 2)), jnp.tile(nsa_pe_v[i], (1, 2)),
                               w1dup(nsa_cmp_k_w1[i]), w1dup(nsa_cmp_v_w1[i]),
                               w2pad(nsa_cmp_k_w2[i]), w2pad(nsa_cmp_v_w2[i]), bsz, t_len)
        o_nsa = _nsa(q_r, proj, kcmp, vcmp, ks_r, vst0, vst1, kw_r, vwt0, vwt1, ovl_t,
                     _tile_row(nsa_norm[i], 2).reshape(LANES, 1), bsz, t_len)
        o_sb = _sb(sq_b, sk_b, sv_b, _tile_row(sb_norm[i], 2), bsz, t_len)

        w_o = _permute_static(w_out[i].astype(BF16), mix_rows, axis=0)
        h = _mix(h, o_gdn, o_nsa, o_sb, w_o[:D_GDN], w_o[D_GDN:D_GDN + D_NSA], w_o[D_GDN + D_NSA:])

        h = _ffn(h, ln_ffn[i].reshape(1, d_model), w_up[i].astype(BF16), ffn_conv[i], w_down[i].astype(BF16), t_len)

        h = _ple(h, p[i].reshape(n, -1), ln_ple[i].reshape(1, d_model), w_ple_gate[i].astype(BF16),
                 w_ple[i].astype(BF16), ple_norm[i].reshape(1, d_model), ln_final.reshape(1, d_model),
                 final_norm=(i == depth - 1))
    return h.reshape(bsz, t_len, d_model)
```

```python
import functools

import jax
import jax.numpy as jnp
import numpy as np
from jax import lax
from jax.experimental import pallas as pl
from jax.experimental.pallas import tpu as pltpu

F32 = jnp.float32
BF16 = jnp.bfloat16
HIGHEST = lax.Precision.HIGHEST

LANES = 128
HEAD_DIM = 64
N_GDN = 6
N_NSA = 6
N_NSA_KV = 2
NSA_REP = N_NSA // N_NSA_KV
N_SB = 4
D_GDN = N_GDN * HEAD_DIM
D_NSA = N_NSA * HEAD_DIM
D_SB = N_SB * HEAD_DIM
GDN_CONV = 4
GDN_BLOCK = 128
GDN_SUBBLOCKS = 4
CMP_LEN = 32
CMP_STRIDE = 16
SEL_BLOCK = 64
SEL_TOPK = 16
WINDOW = 512
Q_BLOCK = 128
NSA_TILES = 4
FFN_CONV = 3
ROPE_THETA = 10000.0
EPS = 1e-6
NEG = -1e30
FORCE_SCORE = 1e4
EXP_UNDERFLOW = -104.0

CB_GQ, CB_GK, CB_GV, CB_GZ = 0, 3, 6, 9
CB_NQ = 12
CB_SMALL = 15
CB_KC, CB_VC, CB_KS, CB_VS, CB_KW, CB_VW = 16, 17, 18, 19, 20, 21
CB_SQ, CB_SK, CB_SV = 22, 24, 26
N_CB = 28
NP_IN = N_CB * LANES
SMALL_A, SMALL_B, SMALL_GATE = 0, 6, 12

NN_DIMS = (((1,), (0,)), ((), ()))
NT_DIMS = (((1,), (1,)), ((), ()))
TN_DIMS = (((0,), (0,)), ((), ()))


def _cparams(n_axes, vmem_mb):
    return pltpu.CompilerParams(dimension_semantics=("arbitrary",) * n_axes,
                                vmem_limit_bytes=vmem_mb * 1024 * 1024)


def _inproj_columns():
    d_gdn, d_nsa, d_kv, d_sb = D_GDN, D_NSA, N_NSA_KV * HEAD_DIM, D_SB
    o = {}
    off = 0
    for name, size in (("gq", d_gdn), ("gk", d_gdn), ("gv", d_gdn), ("gz", d_gdn), ("ga", N_GDN), ("gb", N_GDN),
                       ("nq", d_nsa), ("kc", d_kv), ("vc", d_kv), ("ks", d_kv), ("vs", d_kv), ("kw", d_kv),
                       ("vw", d_kv), ("ng", 3 * N_NSA), ("sq", d_sb), ("sk", d_sb), ("sv", d_sb)):
        o[name] = off
        off += size
    cols = -np.ones((NP_IN,), np.int64)

    def put(cb, lane, src, size):
        cols[cb * LANES + lane: cb * LANES + lane + size] = np.arange(src, src + size)

    put(CB_GQ, 0, o["gq"], d_gdn)
    put(CB_GK, 0, o["gk"], d_gdn)
    put(CB_GV, 0, o["gv"], d_gdn)
    put(CB_GZ, 0, o["gz"], d_gdn)
    for r in range(NSA_REP):
        for g in range(N_NSA_KV):
            put(CB_NQ + r, g * HEAD_DIM, o["nq"] + (g * NSA_REP + r) * HEAD_DIM, HEAD_DIM)
    put(CB_SMALL, SMALL_A, o["ga"], N_GDN)
    put(CB_SMALL, SMALL_B, o["gb"], N_GDN)
    put(CB_SMALL, SMALL_GATE, o["ng"], 3 * N_NSA)
    for cb, nm in ((CB_KC, "kc"), (CB_VC, "vc"), (CB_KS, "ks"), (CB_VS, "vs"), (CB_KW, "kw"), (CB_VW, "vw")):
        put(cb, 0, o[nm], d_kv)
    put(CB_SQ, 0, o["sq"], d_sb)
    put(CB_SK, 0, o["sk"], d_sb)
    put(CB_SV, 0, o["sv"], d_sb)
    return cols


def _mix_rows():
    rows = list(range(D_GDN))
    for c in range(D_NSA):
        r, half, d = c // LANES, (c % LANES) // HEAD_DIM, c % HEAD_DIM
        rows.append(D_GDN + (half * NSA_REP + r) * HEAD_DIM + d)
    rows += list(range(D_GDN + D_NSA, D_GDN + D_NSA + D_SB))
    return np.asarray(rows, np.int64)


def _dot_bf16(a, b, dims=NN_DIMS):
    return lax.dot_general(a.astype(BF16), b.astype(BF16), dims, preferred_element_type=F32)


def _dot_exact01(m01_bf16, x):
    h1 = x.astype(BF16)
    r1 = x - h1.astype(F32)
    h2 = r1.astype(BF16)
    h3 = (r1 - h2.astype(F32)).astype(BF16)
    d = lambda y: jnp.dot(m01_bf16, y, preferred_element_type=F32)
    return (d(h3) + d(h2)) + d(h1)


def _dot_exact01_rhs(x, m01_bf16):
    h1 = x.astype(BF16)
    r1 = x - h1.astype(F32)
    h2 = r1.astype(BF16)
    h3 = (r1 - h2.astype(F32)).astype(BF16)
    d = lambda y: jnp.dot(y, m01_bf16, preferred_element_type=F32)
    return (d(h3) + d(h2)) + d(h1)


def _permute_static(w, index, axis):
    index = np.asarray(index)
    pieces, start = [], 0
    while start < len(index):
        stop = start + 1
        if index[start] < 0:
            while stop < len(index) and index[stop] < 0:
                stop += 1
            shape = list(w.shape)
            shape[axis] = stop - start
            pieces.append(jnp.zeros(shape, w.dtype))
        else:
            while stop < len(index) and index[stop] == index[stop - 1] + 1:
                stop += 1
            pieces.append(lax.slice_in_dim(w, int(index[start]), int(index[stop - 1]) + 1, axis=axis))
        start = stop
    return jnp.concatenate(pieces, axis=axis)


def _iota(shape, dim):
    return lax.broadcasted_iota(jnp.int32, shape, dim)


def _silu(x):
    return x * jax.nn.sigmoid(x)


def _head_sum(x, lo):
    s0 = jnp.sum(jnp.where(lo, x, 0.0), axis=1, keepdims=True)
    s1 = jnp.sum(jnp.where(lo, 0.0, x), axis=1, keepdims=True)
    return jnp.where(lo, s0, s1)


def _head_rmsnorm(x, w_row, lo):
    return x * lax.rsqrt(_head_sum(x * x, lo) * (1.0 / HEAD_DIM) + EPS) * w_row


def _rmsnorm_rows(x, w_row):
    return x * lax.rsqrt(jnp.mean(x * x, axis=-1, keepdims=True) + EPS) * w_row


def _inproj_kernel(x_ref, g_ref, w_ref, o_ref, xn_ref):
    @pl.when(pl.program_id(1) == 0)
    def _():
        xn_ref[...] = _rmsnorm_rows(x_ref[...], g_ref[...]).astype(BF16)

    o_ref[...] = jnp.dot(xn_ref[...], w_ref[...], preferred_element_type=F32)


def _inproj(h, gain, w_bf16, tm=512, tn=1792):
    n, d = h.shape
    return pl.pallas_call(
        _inproj_kernel,
        grid=(n // tm, NP_IN // tn),
        in_specs=[pl.BlockSpec((tm, d), lambda i, j: (i, 0)),
                  pl.BlockSpec((1, d), lambda i, j: (0, 0)),
                  pl.BlockSpec((d, tn), lambda i, j: (0, j))],
        out_specs=pl.BlockSpec((tm, tn), lambda i, j: (i, j)),
        out_shape=jax.ShapeDtypeStruct((n, NP_IN), F32),
        scratch_shapes=[pltpu.VMEM((tm, d), BF16)],
        compiler_params=_cparams(2, 40),
        name="inproj",
    )(h, gain, w_bf16)


def _gdn_kernel(q_ref, k_ref, v_ref, z_ref, s_ref, qh_ref, kh_ref, vh_ref, cq_ref, ck_ref, cv_ref, alog_ref, dt_ref,
                nw_ref, o_ref, qp, kp, vp, st_ref):
    ti = pl.program_id(1)
    t_blk = q_ref.shape[0]
    blk = GDN_BLOCK
    pad = qh_ref.shape[0]
    n_pairs = N_GDN // 2
    halo_keep = jnp.where(ti == 0, 0.0, 1.0)
    for src, halo, dst in ((q_ref, qh_ref, qp), (k_ref, kh_ref, kp), (v_ref, vh_ref, vp)):
        for hp in range(n_pairs):
            dst[hp, pl.ds(0, pad), :] = halo[:, hp * LANES:(hp + 1) * LANES] * halo_keep
            dst[hp, pl.ds(pad, t_blk), :] = src[:, hp * LANES:(hp + 1) * LANES]

    @pl.when(ti == 0)
    def _():
        st_ref[...] = jnp.zeros(st_ref.shape, F32)

    ri = _iota((blk, LANES), 0)
    ci = _iota((blk, LANES), 1)
    lo = ci < HEAD_DIM
    incl = ri >= ci
    strict = ri > ci
    tri_f = jnp.where(incl, 1.0, 0.0)
    blockdiag = (ri < HEAD_DIM) == lo
    eye_f = jnp.where(ri == ci, 1.0, 0.0)

    def same_block(log2_size):
        return lax.shift_right_logical(ri, log2_size) == lax.shift_right_logical(ci, log2_size)
    scale = HEAD_DIM ** -0.5
    nw = nw_ref[...]

    tri_bf = tri_f.astype(BF16)
    idot = _dot_bf16
    rdot = _dot_bf16

    pairs = range(n_pairs)
    n_sub = GDN_SUBBLOCKS
    units = [(b, hp) for b in range(n_sub) for hp in pairs]
    heads = [(u, j) for u in range(len(units)) for j in range(2)]

    def body(c, states):
        base = pl.multiple_of(c * (n_sub * blk), n_sub * blk)
        r0 = [base + b * blk for b in range(n_sub)]

        def conv(xp, w_ref, u):
            b, hp = units[u]
            acc = None
            for s in range(GDN_CONV):
                term = (xp[hp, pl.ds(r0[b] + pad - (GDN_CONV - 1) + s, blk), :]
                        * w_ref[s:s + 1, hp * LANES:(hp + 1) * LANES])
                acc = term if acc is None else acc + term
            return _silu(acc)

        sm = [s_ref[pl.ds(r0[b], blk), :] for b in range(n_sub)]
        gk, beta = [], []
        for u, j in heads:
            b, hp = units[u]
            h = 2 * hp + j
            a_h = sm[b][:, SMALL_A + h:SMALL_A + h + 1]
            b_h = sm[b][:, SMALL_B + h:SMALL_B + h + 1]
            gk.append(jnp.broadcast_to(-jnp.exp(alog_ref[:, h:h + 1]) * jax.nn.softplus(a_h + dt_ref[:, h:h + 1]),
                                       (blk, LANES)))
            beta.append(jax.nn.sigmoid(b_h))
        g_col = [_dot_exact01(tri_bf, g) for g in gk]

        qn, kn, kb, vb, g_pair, eg_pair, rhs = [], [], [], [], [], [], []
        for u in range(len(units)):
            q = conv(qp, cq_ref, u)
            k = conv(kp, ck_ref, u)
            v = conv(vp, cv_ref, u)
            qn.append(q * lax.rsqrt(_head_sum(q * q, lo) + EPS) * scale)
            kn.append(k * lax.rsqrt(_head_sum(k * k, lo) + EPS))
            g_pair.append(jnp.where(lo, g_col[2 * u], g_col[2 * u + 1]))
            eg_pair.append(jnp.exp(g_pair[u]))
            beta_pair = jnp.where(lo, beta[2 * u], beta[2 * u + 1])
            kb.append(kn[u] * beta_pair)
            vb.append(v * beta_pair)
            kbg_sw = pltpu.roll(kb[u] * eg_pair[u], HEAD_DIM, 1)
            rhs += [jnp.where(lo, vb[u], kbg_sw), jnp.where(lo, kbg_sw, vb[u])]

        mask = [lo if j == 0 else jnp.logical_not(lo) for _, j in heads]
        decay = [jnp.exp(jnp.where(incl, g - g.T, NEG)) for g in g_col]
        kk = [rdot(jnp.where(mask[i], kb[u], 0.0), kn[u], NT_DIMS) for i, (u, _) in enumerate(heads)]
        qk = [rdot(jnp.where(mask[i], qn[u], 0.0), kn[u], NT_DIMS) for i, (u, _) in enumerate(heads)]
        lower = [jnp.where(strict, a * d, 0.0) for a, d in zip(kk, decay)]
        attn = [jnp.where(incl, a * d, 0.0) for a, d in zip(qk, decay)]
        d1 = [jnp.where(same_block(3), a, 0.0) for a in lower]
        d2 = [idot(a, a) for a in d1]
        x = [eye_f - a for a in d1]
        x = [xi + idot(xi, a) for xi, a in zip(x, d2)]
        d4 = [idot(a, a) for a in d2]
        x = [xi + idot(xi, a) for xi, a in zip(x, d4)]
        for lg in range(4, 8):
            off_mask = same_block(lg) & jnp.logical_not(same_block(lg - 1))
            y = [idot(jnp.where(off_mask, a, 0.0), xi) for a, xi in zip(lower, x)]
            x = [xi - idot(xi, yi) for xi, yi in zip(x, y)]
        sol = [idot(xi, r) for xi, r in zip(x, rhs)]
        u_all = [jnp.where(lo, sol[2 * u], sol[2 * u + 1]) for u in range(len(units))]
        w_all = [pltpu.roll(jnp.where(lo, sol[2 * u + 1], sol[2 * u]), HEAD_DIM, 1) for u in range(len(units))]
        g_last = [g[blk - 1:blk, :] for g in g_pair]

        states = list(states)
        for b in range(n_sub):
            us = [b * n_pairs + hp for hp in pairs]
            v_new = [u_all[u] - rdot(w_all[u], states[hp]) for hp, u in zip(pairs, us)]
            inter = [rdot(qn[u] * eg_pair[u], states[hp]) for hp, u in zip(pairs, us)]
            intra = [rdot(attn[2 * u + j], v_new[hp]) for hp, u in zip(pairs, us) for j in range(2)]
            kv = [rdot(kn[u] * jnp.exp(g_last[u] - g_pair[u]), v_new[hp], TN_DIMS) for hp, u in zip(pairs, us)]
            for hp, u in zip(pairs, us):
                o = inter[hp] + jnp.where(lo, intra[2 * hp], intra[2 * hp + 1])
                cs = slice(hp * LANES, (hp + 1) * LANES)
                o_ref[pl.ds(r0[b], blk), cs] = _head_rmsnorm(o, nw, lo) * _silu(z_ref[pl.ds(r0[b], blk), cs])
                states[hp] = states[hp] * jnp.exp(g_last[u]) + jnp.where(blockdiag, kv[hp], 0.0)
        return tuple(states)

    states = lax.fori_loop(0, t_blk // (n_sub * blk), body, tuple(st_ref[hp] for hp in pairs))
    for hp in pairs:
        st_ref[hp] = states[hp]


def _gdn(proj, conv_w, a_log_row, dt_row, norm_row, bsz, t_len, t_blk=1024):
    n = proj.shape[0]
    nt = t_len // t_blk
    pad = 8
    w3 = 3 * LANES
    main = lambda cb: pl.BlockSpec((t_blk, w3), lambda b, t: (b * nt + t, cb // 3))
    halo = lambda cb: pl.BlockSpec((pad, w3), lambda b, t: (jnp.maximum((b * nt + t) * (t_blk // pad) - 1, 0), cb // 3))
    cw = lambda j: pl.BlockSpec((GDN_CONV, w3), lambda b, t: (0, j))
    row = pl.BlockSpec((1, LANES), lambda b, t: (0, 0))
    return pl.pallas_call(
        _gdn_kernel,
        grid=(bsz, nt),
        in_specs=[main(CB_GQ), main(CB_GK), main(CB_GV), main(CB_GZ),
                  pl.BlockSpec((t_blk, LANES), lambda b, t: (b * nt + t, CB_SMALL)),
                  halo(CB_GQ), halo(CB_GK), halo(CB_GV),
                  cw(0), cw(1), cw(2), row, row, row],
        out_specs=pl.BlockSpec((t_blk, w3), lambda b, t: (b * nt + t, 0)),
        out_shape=jax.ShapeDtypeStruct((n, D_GDN), F32),
        scratch_shapes=[pltpu.VMEM((N_GDN // 2, t_blk + pad, LANES), F32)] * 3 +[pltpu.VMEM((N_GDN // 2, LANES, LANES), F32)],
        compiler_params=_cparams(2, 48),
        name="gdn",
    )(proj, proj, proj, proj, proj, proj, proj, proj, conv_w, conv_w, conv_w, a_log_row, dt_row, norm_row)


def _rope_table_kernel(pos_ref, inv_ref, sgn_ref, cos_ref, sin_ref):
    ang = pos_ref[...].astype(F32) * inv_ref[...]
    cos_ref[...] = jnp.cos(ang)
    sin_ref[...] = jnp.sin(ang) * sgn_ref[...]


def _rope_tables(pos_col, inv_row, sgn_row, tm=1024):
    n = pos_col.shape[0]
    row = pl.BlockSpec((1, LANES), lambda i: (0, 0))
    out = pl.BlockSpec((tm, LANES), lambda i: (i, 0))
    return pl.pallas_call(
        _rope_table_kernel,
        grid=(n // tm,),
        in_specs=[pl.BlockSpec((tm, 1), lambda i: (i, 0)), row, row],
        out_specs=[out, out],
        out_shape=[jax.ShapeDtypeStruct((n, LANES), F32)] * 2,
        compiler_params=_cparams(1, 32),
        name="rope_tables",
    )(pos_col, inv_row, sgn_row)


def _prep_kernel(q_ref, kc_ref, ks_ref, vs_ref, kw_ref, vw_ref, sq_ref, sk_ref, sv_ref, cos_ref, sin_ref,
                 qo_ref, kco_ref, kso_ref, kwo_ref, vst0_ref, vst1_ref, vwt0_ref, vwt1_ref, sqo_ref, sko_ref, svo_ref):
    cos = cos_ref[...]
    sin = sin_ref[...]
    first = (_iota(cos.shape, 1) % HEAD_DIM) < (HEAD_DIM // 2)

    def rope(x):
        swapped = jnp.where(first, pltpu.roll(x, LANES - HEAD_DIM // 2, 1), pltpu.roll(x, HEAD_DIM // 2, 1))
        return x * cos + swapped * sin

    for r in range(NSA_REP):
        sl = slice(r * LANES, (r + 1) * LANES)
        qo_ref[:, sl] = rope(q_ref[:, sl]).astype(BF16)
    kco_ref[...] = rope(kc_ref[...])
    kso_ref[...] = rope(ks_ref[...]).astype(BF16)
    kwo_ref[...] = rope(kw_ref[...]).astype(BF16)
    top = _iota((LANES, LANES), 0) < HEAD_DIM
    for src, dst0, dst1 in ((vs_ref, vst0_ref, vst1_ref), (vw_ref, vwt0_ref, vwt1_ref)):
        for a in range(src.shape[0] // LANES):
            vt = src[a * LANES:(a + 1) * LANES, :].T
            dst0[a] = jnp.where(top, vt, 1.0).astype(BF16)
            dst1[a] = jnp.where(top, 1.0, vt).astype(BF16)
    sqo_ref[...] = sq_ref[...].astype(BF16)
    sko_ref[...] = sk_ref[...].astype(BF16)
    svo_ref[...] = sv_ref[...].astype(BF16)


def _prep(proj, cos_t, sin_t, tm=512):
    n = proj.shape[0]
    one = lambda cb: pl.BlockSpec((tm, LANES), lambda i: (i, cb))
    two = lambda cb: pl.BlockSpec((tm, 2 * LANES), lambda i: (i, cb // 2))
    tab = pl.BlockSpec((tm, LANES), lambda i: (i, 0))
    o1 = pl.BlockSpec((tm, LANES), lambda i: (i, 0))
    o2 = pl.BlockSpec((tm, 2 * LANES), lambda i: (i, 0))
    o3 = pl.BlockSpec((tm, 3 * LANES), lambda i: (i, 0))
    s1 = lambda dt: jax.ShapeDtypeStruct((n, LANES), dt)
    s2 = jax.ShapeDtypeStruct((n, 2 * LANES), BF16)
    ot = pl.BlockSpec((tm // LANES, LANES, LANES), lambda i: (i, 0, 0))
    st = jax.ShapeDtypeStruct((n // LANES, LANES, LANES), BF16)
    return pl.pallas_call(
        _prep_kernel,
        grid=(n // tm,),
        in_specs=[pl.BlockSpec((tm, 3 * LANES), lambda i: (i, CB_NQ // 3)),
                  one(CB_KC), one(CB_KS), one(CB_VS), one(CB_KW), one(CB_VW),
                  two(CB_SQ), two(CB_SK), two(CB_SV), tab, tab],
        out_specs=[o3, o1, o1, o1, ot, ot, ot, ot, o2, o2, o2],
        out_shape=[jax.ShapeDtypeStruct((n, 3 * LANES), BF16), s1(F32), s1(BF16), s1(BF16), st, st, st, st,
                   s2, s2, s2],
        compiler_params=_cparams(1, 40),
        name="prep",
    )(proj, proj, proj, proj, proj, proj, proj, proj, proj, cos_t, sin_t)


def _compress_kernel(k_ref, v_ref, pek_ref, pev_ref, w1k_ref, w1v_ref, w2k_ref, w2v_ref, ko_ref, vo_ref, xp):
    t_len = k_ref.shape[0]
    n_out = t_len // CMP_STRIDE
    lo = _iota((n_out, LANES), 1) < HEAD_DIM
    for src, pe_ref, w1_ref, w2_ref, out_ref in ((k_ref, pek_ref, w1k_ref, w2k_ref, ko_ref),
                                                 (v_ref, pev_ref, w1v_ref, w2v_ref, vo_ref)):
        xp[pl.ds(0, t_len), :] = src[...]
        xp[pl.ds(t_len, CMP_STRIDE), :] = jnp.zeros((CMP_STRIDE, LANES), F32)
        z0 = jnp.zeros((n_out, LANES), F32)
        z1 = jnp.zeros((n_out, LANES), F32)
        for l in range(CMP_LEN):
            xl = xp[pl.ds(l, n_out, stride=CMP_STRIDE), :] + pe_ref[l:l + 1, :]
            w = w1_ref[l]
            z0 = z0 + jnp.dot(jnp.where(lo, xl, 0.0).astype(BF16), w, preferred_element_type=F32)
            z1 = z1 + jnp.dot(jnp.where(lo, 0.0, xl).astype(BF16), w, preferred_element_type=F32)
        out = (jnp.dot(_silu(z0).astype(BF16), w2_ref[0], preferred_element_type=F32)
               + jnp.dot(_silu(z1).astype(BF16), w2_ref[1], preferred_element_type=F32))
        out_ref[...] = out if out_ref is ko_ref else out.T


def _compress(kc_roped, proj, pe_k2, pe_v2, w1k, w1v, w2k, w2v, bsz, t_len):
    n_out = t_len // CMP_STRIDE
    full = lambda a: pl.BlockSpec(a.shape, lambda b: (0,) * a.ndim)
    out = pl.BlockSpec((n_out, LANES), lambda b: (b, 0))
    return pl.pallas_call(
        _compress_kernel,
        grid=(bsz,),
        in_specs=[pl.BlockSpec((t_len, LANES), lambda b: (b, 0)),
                  pl.BlockSpec((t_len, LANES), lambda b: (b, CB_VC)),
                  full(pe_k2), full(pe_v2), full(w1k), full(w1v), full(w2k), full(w2v)],
        out_specs=[out, pl.BlockSpec((LANES, n_out), lambda b: (b, 0))],
        out_shape=[jax.ShapeDtypeStruct((bsz * n_out, LANES), F32), jax.ShapeDtypeStruct((bsz * LANES, n_out), F32)],
        scratch_shapes=[pltpu.VMEM((t_len + CMP_STRIDE, LANES), F32)],
        compiler_params=_cparams(1, 40),
        name="compress",
    )(kc_roped, proj, pe_k2, pe_v2, w1k, w1v, w2k, w2v)


def _nsa_kernel(q_ref, s_ref, kc_ref, vct_ref, ks_ref, vst0_ref, vst1_ref, kw_ref, vwt0_ref, vwt1_ref, ovl_ref,
                nw_ref, o_ref, acc_ref, raw_a, raw_b, pk_a, pk_b):
    qi = pl.program_id(1)
    q0 = qi * Q_BLOCK
    qb = Q_BLOCK
    cols3 = NSA_REP * qb
    n_cmp = kc_ref.shape[0]
    n_sel = ks_ref.shape[0] // SEL_BLOCK

    ri = _iota((qb, LANES), 0)
    ci = _iota((qb, LANES), 1)
    top = ri < HEAD_DIM
    tq_lane = q0 + ci

    def per_head(fn, x):
        return jnp.concatenate([fn(x[:, r * qb:(r + 1) * qb]) for r in range(NSA_REP)], axis=1)

    q_t = [(q_ref[:, r * LANES:(r + 1) * LANES].astype(F32) * (HEAD_DIM ** -0.5)).T for r in range(NSA_REP)]
    q_grp = [jnp.concatenate([jnp.where(top if g == 0 else jnp.logical_not(top), t, 0.0) for t in q_t],
                             axis=1).astype(BF16) for g in range(N_NSA_KV)]

    kc = kc_ref[...].astype(BF16)
    vct = vct_ref[...].astype(BF16)
    m_c = (_iota((n_cmp, cols3), 0) * CMP_STRIDE + (CMP_LEN - 1)
           <= q0 + (_iota((n_cmp, cols3), 1) & (qb - 1)))
    o_c, imp_t = [], []
    for g in range(N_NSA_KV):
        s = jnp.dot(kc, q_grp[g], preferred_element_type=F32)
        s = jnp.where(m_c, s, NEG)
        e = jnp.where(m_c, jnp.exp(s - jnp.max(s, axis=0, keepdims=True)), 0.0)
        p = e / jnp.maximum(jnp.sum(e, axis=0, keepdims=True), 1e-30)
        o_c.append(jnp.dot(vct, p.astype(BF16), preferred_element_type=F32))
        p_sum = p[:, 0:qb] + p[:, qb:2 * qb] + p[:, 2 * qb:3 * qb]
        imp_t.append(_dot_exact01(ovl_ref[...].astype(BF16), p_sum))

    blk = ri
    tq_lane = q0 + ci
    cur = lax.shift_right_logical(tq_lane, 6)
    forced = (blk == 0) | (blk == cur) | (blk == cur - 1)
    visible = blk * SEL_BLOCK <= tq_lane
    sel = []
    for g in range(N_NSA_KV):
        score = jnp.where(forced, FORCE_SCORE, jnp.where(visible, imp_t[g], -1.0))
        slabs = [score[8 * v:8 * v + 8, :] for v in range(n_sel // 8)]
        cnts = [jnp.zeros((8, LANES), F32) for _ in slabs]
        row8 = _iota((8, LANES), 0)
        for i in range(n_sel):
            si = jnp.broadcast_to(score[i:i + 1, :], (8, LANES))
            for v, slab in enumerate(slabs):
                if 8 * v > i:
                    beats = si >= slab
                elif 8 * v + 7 < i:
                    beats = si > slab
                else:
                    beats = (si > slab) | ((si == slab) & (row8 > i - 8 * v))
                cnts[v] = cnts[v] + jnp.where(beats, 1.0, 0.0)
        picked_rows = [jnp.where(c < float(SEL_TOPK), 1.0, 0.0) for c in cnts]
        if n_sel < LANES:
            picked_rows.append(jnp.zeros((LANES - n_sel, LANES), F32))
        sel.append(jnp.concatenate(picked_rows, axis=0).astype(BF16))

    groups = range(N_NSA_KV)

    def tiles_step(k_ref_, vt_refs, tile_ids, valids, mask_fn, m_run):
        kts, kpos = [], []
        for t, ok in zip(tile_ids, valids):
            k0 = pl.multiple_of(t * LANES, LANES)
            kts.append(k_ref_[pl.ds(k0, LANES), :])
            kpos.append((k0 if ok is None else jnp.where(ok, k0, 1 << 30)) + ri)
        raw = [[jnp.dot(kt, q_grp[g], preferred_element_type=F32) for kt in kts] for g in groups]
        msks = [[mask_fn(g, t, kp) for t, kp in zip(tile_ids, kpos)] for g in groups]
        return softmax_update(raw, msks, vt_refs, tile_ids, m_run)

    def softmax_update(raw, msks, vt_refs, tile_ids, m_run):
        m_out = []
        for g in groups:
            ss = [per_head(lambda x, mk=mk: jnp.where(mk, x, NEG), s) for s, mk in zip(raw[g], msks[g])]
            m_new = m_run[g]
            for s in ss:
                m_new = jnp.maximum(m_new, jnp.max(s, axis=0, keepdims=True))
            acc = acc_ref[g] * jnp.exp(m_run[g] - m_new)
            ps = [per_head(lambda x, mk=mk: jnp.where(mk, x, 0.0), jnp.exp(s - m_new)).astype(BF16)
                  for s, mk in zip(ss, msks[g])]
            for t, p in zip(tile_ids, ps):
                acc = acc + jnp.dot(vt_refs[g][t], p, preferred_element_type=F32)
            acc_ref[g] = acc
            m_out.append(m_new)
        return tuple(m_out)

    def normalised():
        outs = []
        for g in range(N_NSA_KV):
            acc = acc_ref[g]
            denom = acc[HEAD_DIM:HEAD_DIM + 1, :] if g == 0 else acc[0:1, :]
            outs.append(acc / jnp.maximum(denom, 1e-30))
        return outs

    def reset():
        for g in range(N_NSA_KV):
            acc_ref[g] = jnp.zeros((LANES, cols3), F32)
        return tuple(jnp.full((1, cols3), NEG, F32) for _ in range(N_NSA_KV))

    def win_mask(g, t, kpos):
        diff = tq_lane - kpos
        return (diff >= 0) & (diff < WINDOW)

    def sel_tiles(j):
        return [NSA_TILES * j + a for a in range(NSA_TILES)]

    buf_a, buf_b = (raw_a, pk_a), (raw_b, pk_b)

    def sel_issue(j, buf):
        raw_ref, pk_ref = buf
        for a, t in enumerate(sel_tiles(j)):
            kt = ks_ref[pl.ds(pl.multiple_of(t * LANES, LANES), LANES), :]
            expand = jnp.where(lax.shift_right_logical(ri, 6) + 2 * t == ci, 1.0, 0.0).astype(BF16)
            for g in groups:
                raw_ref[g, a] = jnp.dot(kt, q_grp[g], preferred_element_type=F32)
                pk_ref[g, a] = jnp.dot(expand, sel[g], preferred_element_type=F32)

    def sel_consume(j, buf, m_run, diagonal):
        raw_ref, pk_ref = buf
        tiles = sel_tiles(j)
        raw = [[raw_ref[g, a] for a in range(NSA_TILES)] for g in groups]
        msks = []
        for g in groups:
            row = []
            for a, t in enumerate(tiles):
                mk = pk_ref[g, a] > 0.5
                if diagonal:
                    mk = mk & (t * LANES + ri <= tq_lane)
                row.append(mk)
            msks.append(row)
        return softmax_update(raw, msks, (vst0_ref, vst1_ref), tiles, m_run)

    last = qi // NSA_TILES
    sel_issue(0, buf_a)

    def sel_body(i, m_run):
        sel_issue(2 * i + 1, buf_b)
        m_run = sel_consume(2 * i, buf_a, m_run, False)
        sel_issue(2 * i + 2, buf_a)
        return sel_consume(2 * i + 1, buf_b, m_run, False)

    m_sel = lax.fori_loop(0, last // 2, sel_body, reset())

    def odd_tail(m_run):
        sel_issue(last, buf_b)
        m_run = sel_consume(last - 1, buf_a, m_run, False)
        sel_consume(last, buf_b, m_run, True)
        return 0

    def even_tail(m_run):
        sel_consume(last, buf_a, m_run, True)
        return 0

    lax.cond((last & 1) == 1, odd_tail, even_tail, m_sel)
    o_s = normalised()
    m0 = reset()
    w_tiles = [qi - WINDOW // LANES + a for a in range(WINDOW // LANES + 1)]
    tiles_step(kw_ref, (vwt0_ref, vwt1_ref), [jnp.maximum(t, 0) for t in w_tiles], [t >= 0 for t in w_tiles],
               win_mask, m0)
    o_w = normalised()

    gate_t = jax.nn.sigmoid(s_ref[...]).T
    nw_col = nw_ref[...]
    for r in range(NSA_REP):
        cs = slice(r * qb, (r + 1) * qb)
        comb = []
        for g in range(N_NSA_KV):
            h = g * NSA_REP + r
            gc, gs, gw = (gate_t[SMALL_GATE + br * N_NSA + h: SMALL_GATE + br * N_NSA + h + 1, :] for br in range(3))
            comb.append(gc * o_c[g][:, cs] + gs * o_s[g][:, cs] + gw * o_w[g][:, cs])
        o_t = jnp.where(top, comb[0], comb[1])
        sq = o_t * o_t
        ms0 = jnp.sum(jnp.where(top, sq, 0.0), axis=0, keepdims=True)
        ms1 = jnp.sum(jnp.where(top, 0.0, sq), axis=0, keepdims=True)
        o_t = o_t * lax.rsqrt(jnp.where(top, ms0, ms1) * (1.0 / HEAD_DIM) + EPS) * nw_col
        o_ref[:, r * LANES:(r + 1) * LANES] = o_t.T


def _nsa(q_roped, proj, kcmp, vcmp_t, ks, vst0, vst1, kw, vwt0, vwt1, ovl_t, norm_col, bsz, t_len):
    n = proj.shape[0]
    nq = t_len // Q_BLOCK
    n_cmp = t_len // CMP_STRIDE
    per_b = lambda rows: pl.BlockSpec((rows, LANES), lambda b, i: (b, 0))
    tiles = pl.BlockSpec((t_len // LANES, LANES, LANES), lambda b, i: (b, 0, 0))
    return pl.pallas_call(
        _nsa_kernel,
        grid=(bsz, nq),
        in_specs=[pl.BlockSpec((Q_BLOCK, 3 * LANES), lambda b, i: (b * nq + i, 0)),
                  pl.BlockSpec((Q_BLOCK, LANES), lambda b, i: (b * nq + i, CB_SMALL)),
                  per_b(n_cmp), pl.BlockSpec((LANES, n_cmp), lambda b, i: (b, 0)),
                  per_b(t_len), tiles, tiles, per_b(t_len), tiles, tiles,
                  pl.BlockSpec(ovl_t.shape, lambda b, i: (0, 0)),
                  pl.BlockSpec((LANES, 1), lambda b, i: (0, 0))],
        out_specs=pl.BlockSpec((Q_BLOCK, 3 * LANES), lambda b, i: (b * nq + i, 0)),
        out_shape=jax.ShapeDtypeStruct((n, D_NSA), F32),
        scratch_shapes=[pltpu.VMEM((N_NSA_KV, LANES, NSA_REP * Q_BLOCK), F32),
                        pltpu.VMEM((N_NSA_KV, NSA_TILES, LANES, NSA_REP * Q_BLOCK), F32),
                        pltpu.VMEM((N_NSA_KV, NSA_TILES, LANES, NSA_REP * Q_BLOCK), F32),
                        pltpu.VMEM((N_NSA_KV, NSA_TILES, LANES, Q_BLOCK), F32),
                        pltpu.VMEM((N_NSA_KV, NSA_TILES, LANES, Q_BLOCK), F32)],
        compiler_params=_cparams(2, 48),
        name="nsa",
    )(q_roped, proj, kcmp, vcmp_t, ks, vst0, vst1, kw, vwt0, vwt1, ovl_t, norm_col)


def _sb_kernel(q_ref, k_ref, v_ref, nw_ref, o_ref):
    qi = pl.program_id(1)
    qb = Q_BLOCK
    n_pairs = N_SB // 2
    ri = _iota((qb, LANES), 0)
    ci = _iota((qb, LANES), 1)
    lo = ci < HEAD_DIM
    upper_ones = jnp.concatenate([jnp.where(ri > ci, 1.0, 0.0), jnp.ones((qb, LANES), F32)], axis=1).astype(BF16)
    diag_strict = ci < ri
    zero_bf = jnp.zeros((qb, LANES), BF16)
    q_heads = []
    for p in range(n_pairs):
        q = q_ref[:, p * LANES:(p + 1) * LANES] * BF16(HEAD_DIM ** -0.5)
        q_heads += [jnp.where(lo if j == 0 else jnp.logical_not(lo), q, zero_bf) for j in range(2)]
    n_heads = len(q_heads)

    def tile_pair(jt, carries, accs, first_on_diag):
        units = []
        vts = {}
        for slot in range(2):
            t = jt - slot
            ok = t >= 0
            k0 = pl.multiple_of(jnp.maximum(t, 0) * LANES, LANES)
            keep = jnp.where(ok, 1.0, 0.0)
            for p in range(n_pairs):
                kt = k_ref[pl.ds(k0, LANES), p * LANES:(p + 1) * LANES]
                vts[slot, p] = v_ref[pl.ds(k0, LANES), p * LANES:(p + 1) * LANES]
                for j in range(2):
                    h = 2 * p + j
                    units.append((h, slot, keep, lax.dot_general(q_heads[h], kt, NT_DIMS, preferred_element_type=F32)))
        log_1ms, sums = [], []
        for j, slot, keep, z in units:
            log_1m = -(jnp.maximum(z, 0.0) + jnp.log1p(jnp.exp(-jnp.abs(z)))) * keep
            if first_on_diag and slot == 0:
                log_1m = jnp.where(diag_strict, log_1m, 0.0)
            log_1ms.append(log_1m)
            sums.append(_dot_exact01_rhs(log_1m, upper_ones))
        carries, accs = list(carries), list(accs)
        weights = []
        for (j, slot, keep, z), log_1m, sm in zip(units, log_1ms, sums):
            a = jnp.exp(z + log_1m + sm[:, :LANES] + carries[j]) * keep
            if first_on_diag and slot == 0:
                a = jnp.where(diag_strict, a, 0.0)
            weights.append(a.astype(BF16))
            carries[j] = carries[j] + sm[:, LANES:]
        for (j, slot, _, _), a in zip(units, weights):
            accs[j] = accs[j] + jnp.dot(a, vts[slot, j // 2], preferred_element_type=F32)
        return tuple(carries), tuple(accs)

    zeros = (jnp.zeros((qb, LANES), F32),) * n_heads
    carries, accs = tile_pair(qi, zeros, zeros, True)

    def cond(c):
        jt, carries_, _ = c
        worst = carries_[0]
        for c_h in carries_[1:]:
            worst = jnp.maximum(worst, c_h)
        return jnp.logical_and(jt >= 0, jnp.max(worst) > EXP_UNDERFLOW)

    def body(c):
        jt, carries_, accs_ = c
        carries_, accs_ = tile_pair(jt, carries_, accs_, False)
        return jt - 2, carries_, accs_

    _, _, outs = lax.while_loop(cond, body, (qi - 2, carries, accs))
    for p in range(n_pairs):
        o_ref[:, p * LANES:(p + 1) * LANES] = _head_rmsnorm(jnp.where(lo, outs[2 * p], outs[2 * p + 1]), nw_ref[...], lo)


def _sb(sq, sk, sv, norm_row, bsz, t_len):
    n = sq.shape[0]
    nq = t_len // Q_BLOCK
    return pl.pallas_call(
        _sb_kernel,
        grid=(bsz, nq),
        in_specs=[pl.BlockSpec((Q_BLOCK, D_SB), lambda b, i: (b * nq + i, 0)),
                  pl.BlockSpec((t_len, D_SB), lambda b, i: (b, 0)),
                  pl.BlockSpec((t_len, D_SB), lambda b, i: (b, 0)),
                  pl.BlockSpec((1, LANES), lambda b, i: (0, 0))],
        out_specs=pl.BlockSpec((Q_BLOCK, D_SB), lambda b, i: (b * nq + i, 0)),
        out_shape=jax.ShapeDtypeStruct((n, D_SB), F32),
        compiler_params=_cparams(2, 32),
        name="sb",
    )(sq, sk, sv, norm_row)


def _mix_kernel(h_ref, og_ref, on_ref, os_ref, wg_ref, wn_ref, ws_ref, o_ref):
    acc = h_ref[...]
    acc = acc + jnp.dot(og_ref[...].astype(BF16), wg_ref[...], preferred_element_type=F32)
    acc = acc + jnp.dot(on_ref[...].astype(BF16), wn_ref[...], preferred_element_type=F32)
    acc = acc + jnp.dot(os_ref[...].astype(BF16), ws_ref[...], preferred_element_type=F32)
    o_ref[...] = acc


def _mix(h, o_gdn, o_nsa, o_sb, wg, wn, ws, tm=512):
    n, d = h.shape
    rows = lambda a: pl.BlockSpec((tm, a.shape[1]), lambda i: (i, 0))
    full = lambda a: pl.BlockSpec(a.shape, lambda i: (0, 0))
    return pl.pallas_call(
        _mix_kernel,
        grid=(n // tm,),
        in_specs=[rows(h), rows(o_gdn), rows(o_nsa), rows(o_sb), full(wg), full(wn), full(ws)],
        out_specs=rows(h),
        out_shape=jax.ShapeDtypeStruct((n, d), F32),
        compiler_params=_cparams(1, 40),
        name="mix",
    )(h, o_gdn, o_nsa, o_sb, wg, wn, ws)


def _ffn_kernel(h_ref, halo_ref, g_ref, wg_ref, wu_ref, cg_ref, cu_ref, wd_ref, o_ref, xn_ref, acc_ref, *, t_len):
    i = pl.program_id(0)
    j = pl.program_id(1)
    tm = h_ref.shape[0]
    pad = halo_ref.shape[0]

    @pl.when(j == 0)
    def _():
        halo_keep = jnp.where((i * tm) % t_len == 0, 0.0, 1.0)
        xn_ref[pl.ds(0, pad), :] = (_rmsnorm_rows(halo_ref[...], g_ref[...]) * halo_keep).astype(BF16)
        xn_ref[pl.ds(pad, tm), :] = _rmsnorm_rows(h_ref[...], g_ref[...]).astype(BF16)
        acc_ref[...] = jnp.zeros(acc_ref.shape, F32)

    xn = xn_ref[...]

    def conv(w_ref, c_ref):
        u = jnp.dot(xn, w_ref[...], preferred_element_type=F32)
        out = None
        for s in range(FFN_CONV):
            term = u[pad - (FFN_CONV - 1) + s: pad - (FFN_CONV - 1) + s + tm, :] * c_ref[s:s + 1, :]
            out = term if out is None else out + term
        return out

    act = _silu(conv(wg_ref, cg_ref)) * conv(wu_ref, cu_ref)
    acc_ref[...] += jnp.dot(act.astype(BF16), wd_ref[...], preferred_element_type=F32)

    @pl.when(j == pl.num_programs(1) - 1)
    def _():
        o_ref[...] = h_ref[...] + acc_ref[...]


def _ffn(h, gain, w_up, conv_w, w_down, t_len, tm=512, tf=1408):
    n, d = h.shape
    d_ff = w_down.shape[0]
    nf = d_ff // tf
    pad = 16
    return pl.pallas_call(
        functools.partial(_ffn_kernel, t_len=t_len),
        grid=(n // tm, nf),
        in_specs=[pl.BlockSpec((tm, d), lambda i, j: (i, 0)),
                  pl.BlockSpec((pad, d), lambda i, j: (jnp.maximum(i * (tm // pad) - 1, 0), 0)),
                  pl.BlockSpec((1, d), lambda i, j: (0, 0)),
                  pl.BlockSpec((d, tf), lambda i, j: (0, j)),
                  pl.BlockSpec((d, tf), lambda i, j: (0, nf + j)),
                  pl.BlockSpec((FFN_CONV, tf), lambda i, j: (0, j)),
                  pl.BlockSpec((FFN_CONV, tf), lambda i, j: (0, nf + j)),
                  pl.BlockSpec((tf, d), lambda i, j: (j, 0))],
        out_specs=pl.BlockSpec((tm, d), lambda i, j: (i, 0)),
        out_shape=jax.ShapeDtypeStruct((n, d), F32),
        scratch_shapes=[pltpu.VMEM((tm + pad, d), BF16), pltpu.VMEM((tm, d), F32)],
        compiler_params=_cparams(2, 56),
        name="ffn",
    )(h, h, gain, w_up, w_up, conv_w, conv_w, w_down)


def _ple_kernel(h_ref, p_ref, g_ref, wg_ref, wp_ref, pn_ref, fin_ref, o_ref, *, final_norm):
    h = h_ref[...]
    gate = jax.nn.sigmoid(jnp.dot(_rmsnorm_rows(h, g_ref[...]).astype(BF16), wg_ref[...], preferred_element_type=F32))
    emb = jnp.dot(p_ref[...].astype(BF16), wp_ref[...], preferred_element_type=F32)
    out = h + gate * _rmsnorm_rows(emb, pn_ref[...])
    if final_norm:
        out = _rmsnorm_rows(out, fin_ref[...])
    o_ref[...] = out


def _ple(h, p, gain, w_gate, w_ple, ple_gain, fin_gain, final_norm, tm=512):
    n, d = h.shape
    rows = lambda a: pl.BlockSpec((tm, a.shape[1]), lambda i: (i, 0))
    full = lambda a: pl.BlockSpec(a.shape, lambda i: (0, 0))
    return pl.pallas_call(
        functools.partial(_ple_kernel, final_norm=final_norm),
        grid=(n // tm,),
        in_specs=[rows(h), rows(p), full(gain), full(w_gate), full(w_ple), full(ple_gain), full(fin_gain)],
        out_specs=rows(h),
        out_shape=jax.ShapeDtypeStruct((n, d), F32),
        compiler_params=_cparams(1, 40),
        name="ple",
    )(h, p, gain, w_gate, w_ple, ple_gain, fin_gain)


def _overlap_t(t_len):
    n_cmp = t_len // CMP_STRIDE
    n_sel = t_len // SEL_BLOCK
    c0 = np.arange(n_cmp) * CMP_STRIDE
    s0 = np.arange(n_sel) * SEL_BLOCK
    ov = np.clip(np.minimum(c0[None, :] + CMP_LEN, s0[:, None] + SEL_BLOCK) - np.maximum(c0[None, :], s0[:, None]), 0, None)
    out = np.zeros((LANES, n_cmp), np.float32)
    out[:n_sel] = ov.astype(np.float32) / CMP_LEN
    return jnp.asarray(out)


def _tile_row(v, reps):
    return jnp.tile(v.astype(F32), reps).reshape(1, -1)


def kernel(x, p, positions, ln_mix, w_in, gdn_conv, gdn_a_log, gdn_dt_bias, gdn_norm, nsa_pe_k, nsa_pe_v, nsa_cmp_k_w1, nsa_cmp_k_w2, nsa_cmp_v_w1, nsa_cmp_v_w2, nsa_norm, sb_norm, w_out, ln_ffn, w_up, ffn_conv, w_down, ln_ple, w_ple_gate, w_ple, ple_norm, ln_final):
    bsz, t_len, d_model = x.shape
    depth = w_in.shape[0]
    n = bsz * t_len
    assert t_len % (NSA_TILES * LANES) == 0 and 2 < t_len // SEL_BLOCK <= LANES

    cols = _inproj_columns()
    mix_rows = _mix_rows()
    ovl_t = _overlap_t(t_len)

    half = HEAD_DIM // 2
    inv = ROPE_THETA ** (-jnp.arange(half, dtype=F32) / half)
    inv_row = jnp.tile(inv, LANES // half).reshape(1, LANES)
    sgn_row = jnp.asarray(np.where((np.arange(LANES) % HEAD_DIM) < half, -1.0, 1.0).astype(np.float32)).reshape(1, LANES)
    cos_t, sin_t = _rope_tables(positions.reshape(n, 1), inv_row, sgn_row)

    pad_lanes = lambda v: jnp.pad(v.astype(F32), (0, LANES - v.shape[0])).reshape(1, LANES)
    h = x.reshape(n, d_model)
    for i in range(depth):
        w_in_p = _permute_static(w_in[i].astype(BF16), cols, axis=1)
        proj = _inproj(h, ln_mix[i].reshape(1, d_model), w_in_p)

        o_gdn = _gdn(proj, gdn_conv[i], pad_lanes(gdn_a_log[i]), pad_lanes(gdn_dt_bias[i]),
                     _tile_row(gdn_norm[i], 2), bsz, t_len)

        q_r, kc_r, ks_r, kw_r, vst0, vst1, vwt0, vwt1, sq_b, sk_b, sv_b = _prep(proj, cos_t, sin_t)
        w1dup = lambda w1: jnp.tile(w1.reshape(CMP_LEN, HEAD_DIM, -1), (1, 2, 1)).astype(BF16)
        w2pad = lambda w2: jnp.stack([jnp.pad(w2, ((0, 0), (0, HEAD_DIM))), jnp.pad(w2, ((0, 0), (HEAD_DIM, 0)))]).astype(BF16)
        kcmp, vcmp = _compress(kc_r, proj, jnp.tile(nsa_pe_k[i], (1, 2)), jnp.tile(nsa_pe_v[i], (1, 2)),
                               w1dup(nsa_cmp_k_w1[i]), w1dup(nsa_cmp_v_w1[i]),
                               w2pad(nsa_cmp_k_w2[i]), w2pad(nsa_cmp_v_w2[i]), bsz, t_len)
        o_nsa = _nsa(q_r, proj, kcmp, vcmp, ks_r, vst0, vst1, kw_r, vwt0, vwt1, ovl_t,
                     _tile_row(nsa_norm[i], 2).reshape(LANES, 1), bsz, t_len)
        o_sb = _sb(sq_b, sk_b, sv_b, _tile_row(sb_norm[i], 2), bsz, t_len)

        w_o = _permute_static(w_out[i].astype(BF16), mix_rows, axis=0)
        h = _mix(h, o_gdn, o_nsa, o_sb, w_o[:D_GDN], w_o[D_GDN:D_GDN + D_NSA], w_o[D_GDN + D_NSA:])

        h = _ffn(h, ln_ffn[i].reshape(1, d_model), w_up[i].astype(BF16), ffn_conv[i], w_down[i].astype(BF16), t_len)

        h = _ple(h, p[i].reshape(n, -1), ln_ple[i].reshape(1, d_model), w_ple_gate[i].astype(BF16),
                 w_ple[i].astype(BF16), ple_norm[i].reshape(1, d_model), ln_final.reshape(1, d_model),
                 final_norm=(i == depth - 1))
    return h.reshape(bsz, t_len, d_model)
```

```python
import functools

import jax
import jax.numpy as jnp
import numpy as np
from jax import lax
from jax.experimental import pallas as pl
from jax.experimental.pallas import tpu as pltpu

F32 = jnp.float32
BF16 = jnp.bfloat16
HIGHEST = lax.Precision.HIGHEST

LANES = 128
HEAD_DIM = 64
N_GDN = 6
N_NSA = 6
N_NSA_KV = 2
NSA_REP = N_NSA // N_NSA_KV
N_SB = 4
D_GDN = N_GDN * HEAD_DIM
D_NSA = N_NSA * HEAD_DIM
D_SB = N_SB * HEAD_DIM
GDN_CONV = 4
GDN_BLOCK = 128
GDN_SUBBLOCKS = 4
CMP_LEN = 32
CMP_STRIDE = 16
SEL_BLOCK = 64
SEL_TOPK = 16
WINDOW = 512
Q_BLOCK = 128
NSA_TILES = 4
SB_FIRST_TILES = 4
FFN_CONV = 3
ROPE_THETA = 10000.0
EPS = 1e-6
NEG = -1e30
FORCE_SCORE = 1e4
EXP_UNDERFLOW = -104.0

CB_GQ, CB_GK, CB_GV, CB_GZ = 0, 3, 6, 9
CB_NQ = 12
CB_SMALL = 15
CB_KC, CB_VC, CB_KS, CB_VS, CB_KW, CB_VW = 16, 17, 18, 19, 20, 21
CB_SQ, CB_SK, CB_SV = 22, 24, 26
N_CB = 28
NP_IN = N_CB * LANES
SMALL_A, SMALL_B, SMALL_GATE = 0, 6, 12

NN_DIMS = (((1,), (0,)), ((), ()))
NT_DIMS = (((1,), (1,)), ((), ()))
TN_DIMS = (((0,), (0,)), ((), ()))


def _cparams(n_axes, vmem_mb):
    return pltpu.CompilerParams(dimension_semantics=("arbitrary",) * n_axes,
                                vmem_limit_bytes=vmem_mb * 1024 * 1024)


def _inproj_columns():
    d_gdn, d_nsa, d_kv, d_sb = D_GDN, D_NSA, N_NSA_KV * HEAD_DIM, D_SB
    o = {}
    off = 0
    for name, size in (("gq", d_gdn), ("gk", d_gdn), ("gv", d_gdn), ("gz", d_gdn), ("ga", N_GDN), ("gb", N_GDN),
                       ("nq", d_nsa), ("kc", d_kv), ("vc", d_kv), ("ks", d_kv), ("vs", d_kv), ("kw", d_kv),
                       ("vw", d_kv), ("ng", 3 * N_NSA), ("sq", d_sb), ("sk", d_sb), ("sv", d_sb)):
        o[name] = off
        off += size
    cols = -np.ones((NP_IN,), np.int64)

    def put(cb, lane, src, size):
        cols[cb * LANES + lane: cb * LANES + lane + size] = np.arange(src, src + size)

    put(CB_GQ, 0, o["gq"], d_gdn)
    put(CB_GK, 0, o["gk"], d_gdn)
    put(CB_GV, 0, o["gv"], d_gdn)
    put(CB_GZ, 0, o["gz"], d_gdn)
    for r in range(NSA_REP):
        for g in range(N_NSA_KV):
            put(CB_NQ + r, g * HEAD_DIM, o["nq"] + (g * NSA_REP + r) * HEAD_DIM, HEAD_DIM)
    put(CB_SMALL, SMALL_A, o["ga"], N_GDN)
    put(CB_SMALL, SMALL_B, o["gb"], N_GDN)
    put(CB_SMALL, SMALL_GATE, o["ng"], 3 * N_NSA)
    for cb, nm in ((CB_KC, "kc"), (CB_VC, "vc"), (CB_KS, "ks"), (CB_VS, "vs"), (CB_KW, "kw"), (CB_VW, "vw")):
        put(cb, 0, o[nm], d_kv)
    put(CB_SQ, 0, o["sq"], d_sb)
    put(CB_SK, 0, o["sk"], d_sb)
    put(CB_SV, 0, o["sv"], d_sb)
    return cols


def _mix_rows():
    rows = list(range(D_GDN))
    for c in range(D_NSA):
        r, half, d = c // LANES, (c % LANES) // HEAD_DIM, c % HEAD_DIM
        rows.append(D_GDN + (half * NSA_REP + r) * HEAD_DIM + d)
    rows += list(range(D_GDN + D_NSA, D_GDN + D_NSA + D_SB))
    return np.asarray(rows, np.int64)


def _dot_bf16(a, b, dims=NN_DIMS):
    return lax.dot_general(a.astype(BF16), b.astype(BF16), dims, preferred_element_type=F32)


def _dot_exact01(m01_bf16, x):
    h1 = x.astype(BF16)
    r1 = x - h1.astype(F32)
    h2 = r1.astype(BF16)
    h3 = (r1 - h2.astype(F32)).astype(BF16)
    d = lambda y: jnp.dot(m01_bf16, y, preferred_element_type=F32)
    return (d(h3) + d(h2)) + d(h1)


def _dot_exact01_rhs(x, m01_bf16):
    h1 = x.astype(BF16)
    r1 = x - h1.astype(F32)
    h2 = r1.astype(BF16)
    h3 = (r1 - h2.astype(F32)).astype(BF16)
    d = lambda y: jnp.dot(y, m01_bf16, preferred_element_type=F32)
    return (d(h3) + d(h2)) + d(h1)


def _permute_static(w, index, axis):
    index = np.asarray(index)
    pieces, start = [], 0
    while start < len(index):
        stop = start + 1
        if index[start] < 0:
            while stop < len(index) and index[stop] < 0:
                stop += 1
            shape = list(w.shape)
            shape[axis] = stop - start
            pieces.append(jnp.zeros(shape, w.dtype))
        else:
            while stop < len(index) and index[stop] == index[stop - 1] + 1:
                stop += 1
            pieces.append(lax.slice_in_dim(w, int(index[start]), int(index[stop - 1]) + 1, axis=axis))
        start = stop
    return jnp.concatenate(pieces, axis=axis)


def _iota(shape, dim):
    return lax.broadcasted_iota(jnp.int32, shape, dim)


def _silu(x):
    return x * jax.nn.sigmoid(x)


def _head_sum(x, lo):
    s0 = jnp.sum(jnp.where(lo, x, 0.0), axis=1, keepdims=True)
    s1 = jnp.sum(jnp.where(lo, 0.0, x), axis=1, keepdims=True)
    return jnp.where(lo, s0, s1)


def _head_rmsnorm(x, w_row, lo):
    return x * lax.rsqrt(_head_sum(x * x, lo) * (1.0 / HEAD_DIM) + EPS) * w_row


def _rmsnorm_rows(x, w_row):
    return x * lax.rsqrt(jnp.mean(x * x, axis=-1, keepdims=True) + EPS) * w_row


def _inproj_kernel(x_ref, g_ref, w_ref, o_ref, xn_ref):
    @pl.when(pl.program_id(1) == 0)
    def _():
        xn_ref[...] = _rmsnorm_rows(x_ref[...], g_ref[...]).astype(BF16)

    o_ref[...] = jnp.dot(xn_ref[...], w_ref[...], preferred_element_type=F32)


def _inproj(h, gain, w_bf16, tm=512, tn=NP_IN):
    n, d = h.shape
    return pl.pallas_call(
        _inproj_kernel,
        grid=(n // tm, NP_IN // tn),
        in_specs=[pl.BlockSpec((tm, d), lambda i, j: (i, 0)),
                  pl.BlockSpec((1, d), lambda i, j: (0, 0)),
                  pl.BlockSpec((d, tn), lambda i, j: (0, j))],
        out_specs=pl.BlockSpec((tm, tn), lambda i, j: (i, j)),
        out_shape=jax.ShapeDtypeStruct((n, NP_IN), F32),
        scratch_shapes=[pltpu.VMEM((tm, d), BF16)],
        compiler_params=_cparams(2, 48),
        name="inproj",
    )(h, gain, w_bf16)


def _gdn_kernel(q_ref, k_ref, v_ref, z_ref, s_ref, qh_ref, kh_ref, vh_ref, cq_ref, ck_ref, cv_ref, alog_ref, dt_ref,
                nw_ref, o_ref, qp, kp, vp, st_ref):
    ti = pl.program_id(1)
    t_blk = q_ref.shape[0]
    blk = GDN_BLOCK
    pad = qh_ref.shape[0]
    n_pairs = N_GDN // 2
    halo_keep = jnp.where(ti == 0, 0.0, 1.0)
    for src, halo, dst in ((q_ref, qh_ref, qp), (k_ref, kh_ref, kp), (v_ref, vh_ref, vp)):
        for hp in range(n_pairs):
            dst[hp, pl.ds(0, pad), :] = halo[:, hp * LANES:(hp + 1) * LANES] * halo_keep
            dst[hp, pl.ds(pad, t_blk), :] = src[:, hp * LANES:(hp + 1) * LANES]

    @pl.when(ti == 0)
    def _():
        st_ref[...] = jnp.zeros(st_ref.shape, F32)

    ri = _iota((blk, LANES), 0)
    ci = _iota((blk, LANES), 1)
    lo = ci < HEAD_DIM
    incl = ri >= ci
    strict = ri > ci
    tri_f = jnp.where(incl, 1.0, 0.0)
    blockdiag = (ri < HEAD_DIM) == lo
    eye_f = jnp.where(ri == ci, 1.0, 0.0)

    def same_block(log2_size):
        return lax.shift_right_logical(ri, log2_size) == lax.shift_right_logical(ci, log2_size)
    scale = HEAD_DIM ** -0.5
    nw = nw_ref[...]

    tri_bf = tri_f.astype(BF16)
    idot = _dot_bf16
    rdot = _dot_bf16

    pairs = range(n_pairs)
    n_sub = GDN_SUBBLOCKS
    units = [(b, hp) for b in range(n_sub) for hp in pairs]
    heads = [(u, j) for u in range(len(units)) for j in range(2)]

    def body(c, states):
        base = pl.multiple_of(c * (n_sub * blk), n_sub * blk)
        r0 = [base + b * blk for b in range(n_sub)]

        def conv(xp, w_ref, u):
            b, hp = units[u]
            acc = None
            for s in range(GDN_CONV):
                term = (xp[hp, pl.ds(r0[b] + pad - (GDN_CONV - 1) + s, blk), :]
                        * w_ref[s:s + 1, hp * LANES:(hp + 1) * LANES])
                acc = term if acc is None else acc + term
            return _silu(acc)

        sm = [s_ref[pl.ds(r0[b], blk), :] for b in range(n_sub)]
        gk, beta = [], []
        for u, j in heads:
            b, hp = units[u]
            h = 2 * hp + j
            a_h = sm[b][:, SMALL_A + h:SMALL_A + h + 1]
            b_h = sm[b][:, SMALL_B + h:SMALL_B + h + 1]
            gk.append(jnp.broadcast_to(-jnp.exp(alog_ref[:, h:h + 1]) * jax.nn.softplus(a_h + dt_ref[:, h:h + 1]),
                                       (blk, LANES)))
            beta.append(jax.nn.sigmoid(b_h))
        g_col = [_dot_exact01(tri_bf, g) for g in gk]

        qn, kn, kb, vb, g_pair, eg_pair, rhs = [], [], [], [], [], [], []
        for u in range(len(units)):
            q = conv(qp, cq_ref, u)
            k = conv(kp, ck_ref, u)
            v = conv(vp, cv_ref, u)
            qn.append(q * lax.rsqrt(_head_sum(q * q, lo) + EPS) * scale)
            kn.append(k * lax.rsqrt(_head_sum(k * k, lo) + EPS))
            g_pair.append(jnp.where(lo, g_col[2 * u], g_col[2 * u + 1]))
            eg_pair.append(jnp.exp(g_pair[u]))
            beta_pair = jnp.where(lo, beta[2 * u], beta[2 * u + 1])
            kb.append(kn[u] * beta_pair)
            vb.append(v * beta_pair)
            kbg_sw = pltpu.roll(kb[u] * eg_pair[u], HEAD_DIM, 1)
            rhs += [jnp.where(lo, vb[u], kbg_sw), jnp.where(lo, kbg_sw, vb[u])]

        mask = [lo if j == 0 else jnp.logical_not(lo) for _, j in heads]
        decay = [jnp.exp(jnp.where(incl, g - g.T, NEG)) for g in g_col]
        kk = [rdot(jnp.where(mask[i], kb[u], 0.0), kn[u], NT_DIMS) for i, (u, _) in enumerate(heads)]
        qk = [rdot(jnp.where(mask[i], qn[u], 0.0), kn[u], NT_DIMS) for i, (u, _) in enumerate(heads)]
        lower = [jnp.where(strict, a * d, 0.0) for a, d in zip(kk, decay)]
        attn = [jnp.where(incl, a * d, 0.0) for a, d in zip(qk, decay)]
        d1 = [jnp.where(same_block(3), a, 0.0) for a in lower]
        d2 = [idot(a, a) for a in d1]
        x = [eye_f - a for a in d1]
        x = [xi + idot(xi, a) for xi, a in zip(x, d2)]
        d4 = [idot(a, a) for a in d2]
        x = [xi + idot(xi, a) for xi, a in zip(x, d4)]
        for lg in range(4, 8):
            off_mask = same_block(lg) & jnp.logical_not(same_block(lg - 1))
            y = [idot(jnp.where(off_mask, a, 0.0), xi) for a, xi in zip(lower, x)]
            x = [xi - idot(xi, yi) for xi, yi in zip(x, y)]
        sol = [idot(xi, r) for xi, r in zip(x, rhs)]
        u_all = [jnp.where(lo, sol[2 * u], sol[2 * u + 1]) for u in range(len(units))]
        w_all = [pltpu.roll(jnp.where(lo, sol[2 * u + 1], sol[2 * u]), HEAD_DIM, 1) for u in range(len(units))]
        g_last = [g[blk - 1:blk, :] for g in g_pair]

        states = list(states)
        for b in range(n_sub):
            us = [b * n_pairs + hp for hp in pairs]
            v_new = [u_all[u] - rdot(w_all[u], states[hp]) for hp, u in zip(pairs, us)]
            inter = [rdot(qn[u] * eg_pair[u], states[hp]) for hp, u in zip(pairs, us)]
            intra = [rdot(attn[2 * u + j], v_new[hp]) for hp, u in zip(pairs, us) for j in range(2)]
            kv = [rdot(kn[u] * jnp.exp(g_last[u] - g_pair[u]), v_new[hp], TN_DIMS) for hp, u in zip(pairs, us)]
            for hp, u in zip(pairs, us):
                o = inter[hp] + jnp.where(lo, intra[2 * hp], intra[2 * hp + 1])
                cs = slice(hp * LANES, (hp + 1) * LANES)
                o_ref[pl.ds(r0[b], blk), cs] = _head_rmsnorm(o, nw, lo) * _silu(z_ref[pl.ds(r0[b], blk), cs])
                states[hp] = states[hp] * jnp.exp(g_last[u]) + jnp.where(blockdiag, kv[hp], 0.0)
        return tuple(states)

    states = lax.fori_loop(0, t_blk // (n_sub * blk), body, tuple(st_ref[hp] for hp in pairs))
    for hp in pairs:
        st_ref[hp] = states[hp]


def _gdn(proj, conv_w, a_log_row, dt_row, norm_row, bsz, t_len, t_blk=1024):
    n = proj.shape[0]
    nt = t_len // t_blk
    pad = 8
    w3 = 3 * LANES
    main = lambda cb: pl.BlockSpec((t_blk, w3), lambda b, t: (b * nt + t, cb // 3))
    halo = lambda cb: pl.BlockSpec((pad, w3), lambda b, t: (jnp.maximum((b * nt + t) * (t_blk // pad) - 1, 0), cb // 3))
    cw = lambda j: pl.BlockSpec((GDN_CONV, w3), lambda b, t: (0, j))
    row = pl.BlockSpec((1, LANES), lambda b, t: (0, 0))
    return pl.pallas_call(
        _gdn_kernel,
        grid=(bsz, nt),
        in_specs=[main(CB_GQ), main(CB_GK), main(CB_GV), main(CB_GZ),
                  pl.BlockSpec((t_blk, LANES), lambda b, t: (b * nt + t, CB_SMALL)),
                  halo(CB_GQ), halo(CB_GK), halo(CB_GV),
                  cw(0), cw(1), cw(2), row, row, row],
        out_specs=pl.BlockSpec((t_blk, w3), lambda b, t: (b * nt + t, 0)),
        out_shape=jax.ShapeDtypeStruct((n, D_GDN), F32),
        scratch_shapes=[pltpu.VMEM((N_GDN // 2, t_blk + pad, LANES), F32)] * 3 +[pltpu.VMEM((N_GDN // 2, LANES, LANES), F32)],
        compiler_params=_cparams(2, 48),
        name="gdn",
    )(proj, proj, proj, proj, proj, proj, proj, proj, conv_w, conv_w, conv_w, a_log_row, dt_row, norm_row)


def _rope_table_kernel(pos_ref, inv_ref, sgn_ref, cos_ref, sin_ref):
    ang = pos_ref[...].astype(F32) * inv_ref[...]
    cos_ref[...] = jnp.cos(ang)
    sin_ref[...] = jnp.sin(ang) * sgn_ref[...]


def _rope_tables(pos_col, inv_row, sgn_row, tm=1024):
    n = pos_col.shape[0]
    row = pl.BlockSpec((1, LANES), lambda i: (0, 0))
    out = pl.BlockSpec((tm, LANES), lambda i: (i, 0))
    return pl.pallas_call(
        _rope_table_kernel,
        grid=(n // tm,),
        in_specs=[pl.BlockSpec((tm, 1), lambda i: (i, 0)), row, row],
        out_specs=[out, out],
        out_shape=[jax.ShapeDtypeStruct((n, LANES), F32)] * 2,
        compiler_params=_cparams(1, 32),
        name="rope_tables",
    )(pos_col, inv_row, sgn_row)


def _prep_kernel(q_ref, kc_ref, ks_ref, vs_ref, kw_ref, vw_ref, sq_ref, sk_ref, sv_ref, cos_ref, sin_ref,
                 qo_ref, kco_ref, kso_ref, kwo_ref, vst0_ref, vst1_ref, vwt0_ref, vwt1_ref, sqo_ref, sko_ref, svo_ref):
    cos = cos_ref[...]
    sin = sin_ref[...]
    first = (_iota(cos.shape, 1) % HEAD_DIM) < (HEAD_DIM // 2)

    def rope(x):
        swapped = jnp.where(first, pltpu.roll(x, LANES - HEAD_DIM // 2, 1), pltpu.roll(x, HEAD_DIM // 2, 1))
        return x * cos + swapped * sin

    for r in range(NSA_REP):
        sl = slice(r * LANES, (r + 1) * LANES)
        qo_ref[:, sl] = rope(q_ref[:, sl]).astype(BF16)
    kco_ref[...] = rope(kc_ref[...])
    kso_ref[...] = rope(ks_ref[...]).astype(BF16)
    kwo_ref[...] = rope(kw_ref[...]).astype(BF16)
    top = _iota((LANES, LANES), 0) < HEAD_DIM
    for src, dst0, dst1 in ((vs_ref, vst0_ref, vst1_ref), (vw_ref, vwt0_ref, vwt1_ref)):
        for a in range(src.shape[0] // LANES):
            vt = src[a * LANES:(a + 1) * LANES, :].T
            dst0[a] = jnp.where(top, vt, 1.0).astype(BF16)
            dst1[a] = jnp.where(top, 1.0, vt).astype(BF16)
    sqo_ref[...] = sq_ref[...].astype(BF16)
    sko_ref[...] = sk_ref[...].astype(BF16)
    svo_ref[...] = sv_ref[...].astype(BF16)


def _prep(proj, cos_t, sin_t, tm=512):
    n = proj.shape[0]
    one = lambda cb: pl.BlockSpec((tm, LANES), lambda i: (i, cb))
    two = lambda cb: pl.BlockSpec((tm, 2 * LANES), lambda i: (i, cb // 2))
    tab = pl.BlockSpec((tm, LANES), lambda i: (i, 0))
    o1 = pl.BlockSpec((tm, LANES), lambda i: (i, 0))
    o2 = pl.BlockSpec((tm, 2 * LANES), lambda i: (i, 0))
    o3 = pl.BlockSpec((tm, 3 * LANES), lambda i: (i, 0))
    s1 = lambda dt: jax.ShapeDtypeStruct((n, LANES), dt)
    s2 = jax.ShapeDtypeStruct((n, 2 * LANES), BF16)
    ot = pl.BlockSpec((tm // LANES, LANES, LANES), lambda i: (i, 0, 0))
    st = jax.ShapeDtypeStruct((n // LANES, LANES, LANES), BF16)
    return pl.pallas_call(
        _prep_kernel,
        grid=(n // tm,),
        in_specs=[pl.BlockSpec((tm, 3 * LANES), lambda i: (i, CB_NQ // 3)),
                  one(CB_KC), one(CB_KS), one(CB_VS), one(CB_KW), one(CB_VW),
                  two(CB_SQ), two(CB_SK), two(CB_SV), tab, tab],
        out_specs=[o3, o1, o1, o1, ot, ot, ot, ot, o2, o2, o2],
        out_shape=[jax.ShapeDtypeStruct((n, 3 * LANES), BF16), s1(F32), s1(BF16), s1(BF16), st, st, st, st,
                   s2, s2, s2],
        compiler_params=_cparams(1, 40),
        name="prep",
    )(proj, proj, proj, proj, proj, proj, proj, proj, proj, cos_t, sin_t)


def _compress_kernel(k_ref, v_ref, pek_ref, pev_ref, w1k_ref, w1v_ref, w2k_ref, w2v_ref, ko_ref, vo_ref, xp):
    t_len = k_ref.shape[0]
    n_out = t_len // CMP_STRIDE
    lo = _iota((n_out, LANES), 1) < HEAD_DIM
    for src, pe_ref, w1_ref, w2_ref, out_ref in ((k_ref, pek_ref, w1k_ref, w2k_ref, ko_ref),
                                                 (v_ref, pev_ref, w1v_ref, w2v_ref, vo_ref)):
        xp[pl.ds(0, t_len), :] = src[...]
        xp[pl.ds(t_len, CMP_STRIDE), :] = jnp.zeros((CMP_STRIDE, LANES), F32)
        z0 = jnp.zeros((n_out, LANES), F32)
        z1 = jnp.zeros((n_out, LANES), F32)
        for l in range(CMP_LEN):
            xl = xp[pl.ds(l, n_out, stride=CMP_STRIDE), :] + pe_ref[l:l + 1, :]
            w = w1_ref[l]
            z0 = z0 + jnp.dot(jnp.where(lo, xl, 0.0).astype(BF16), w, preferred_element_type=F32)
            z1 = z1 + jnp.dot(jnp.where(lo, 0.0, xl).astype(BF16), w, preferred_element_type=F32)
        out = (jnp.dot(_silu(z0).astype(BF16), w2_ref[0], preferred_element_type=F32)
               + jnp.dot(_silu(z1).astype(BF16), w2_ref[1], preferred_element_type=F32))
        out_ref[...] = out if out_ref is ko_ref else out.T


def _compress(kc_roped, proj, pe_k2, pe_v2, w1k, w1v, w2k, w2v, bsz, t_len):
    n_out = t_len // CMP_STRIDE
    full = lambda a: pl.BlockSpec(a.shape, lambda b: (0,) * a.ndim)
    out = pl.BlockSpec((n_out, LANES), lambda b: (b, 0))
    return pl.pallas_call(
        _compress_kernel,
        grid=(bsz,),
        in_specs=[pl.BlockSpec((t_len, LANES), lambda b: (b, 0)),
                  pl.BlockSpec((t_len, LANES), lambda b: (b, CB_VC)),
                  full(pe_k2), full(pe_v2), full(w1k), full(w1v), full(w2k), full(w2v)],
        out_specs=[out, pl.BlockSpec((LANES, n_out), lambda b: (b, 0))],
        out_shape=[jax.ShapeDtypeStruct((bsz * n_out, LANES), F32), jax.ShapeDtypeStruct((bsz * LANES, n_out), F32)],
        scratch_shapes=[pltpu.VMEM((t_len + CMP_STRIDE, LANES), F32)],
        compiler_params=_cparams(1, 40),
        name="compress",
    )(kc_roped, proj, pe_k2, pe_v2, w1k, w1v, w2k, w2v)


def _nsa_kernel(q_ref, s_ref, kc_ref, vct_ref, ks_ref, vst0_ref, vst1_ref, kw_ref, vwt0_ref, vwt1_ref, ovl_ref,
                nw_ref, o_ref, acc_ref, raw_a, raw_b, pk_a, pk_b, raw_w):
    qi = pl.program_id(1)
    q0 = qi * Q_BLOCK
    qb = Q_BLOCK
    cols3 = NSA_REP * qb
    n_cmp = kc_ref.shape[0]
    n_sel = ks_ref.shape[0] // SEL_BLOCK

    ri = _iota((qb, LANES), 0)
    ci = _iota((qb, LANES), 1)
    top = ri < HEAD_DIM
    tq_lane = q0 + ci

    def per_head(fn, x):
        return jnp.concatenate([fn(x[:, r * qb:(r + 1) * qb]) for r in range(NSA_REP)], axis=1)

    q_t = [(q_ref[:, r * LANES:(r + 1) * LANES].astype(F32) * (HEAD_DIM ** -0.5)).T for r in range(NSA_REP)]
    q_grp = [jnp.concatenate([jnp.where(top if g == 0 else jnp.logical_not(top), t, 0.0) for t in q_t],
                             axis=1).astype(BF16) for g in range(N_NSA_KV)]

    w_tiles = [qi - WINDOW // LANES + a for a in range(WINDOW // LANES + 1)]
    w_clamped = [jnp.maximum(t, 0) for t in w_tiles]
    for a, t in enumerate(w_clamped):
        kt = kw_ref[pl.ds(pl.multiple_of(t * LANES, LANES), LANES), :]
        for g in range(N_NSA_KV):
            raw_w[g, a] = jnp.dot(kt, q_grp[g], preferred_element_type=F32)
    for a in range(NSA_TILES):
        kt = ks_ref[pl.ds(a * LANES, LANES), :]
        for g in range(N_NSA_KV):
            raw_a[g, a] = jnp.dot(kt, q_grp[g], preferred_element_type=F32)

    kc = kc_ref[...].astype(BF16)
    vct = vct_ref[...].astype(BF16)
    m_c = (_iota((n_cmp, cols3), 0) * CMP_STRIDE + (CMP_LEN - 1)
           <= q0 + (_iota((n_cmp, cols3), 1) & (qb - 1)))
    o_c, imp_t = [], []
    for g in range(N_NSA_KV):
        s = jnp.dot(kc, q_grp[g], preferred_element_type=F32)
        s = jnp.where(m_c, s, NEG)
        e = jnp.where(m_c, jnp.exp(s - jnp.max(s, axis=0, keepdims=True)), 0.0)
        p = e / jnp.maximum(jnp.sum(e, axis=0, keepdims=True), 1e-30)
        o_c.append(jnp.dot(vct, p.astype(BF16), preferred_element_type=F32))
        p_sum = p[:, 0:qb] + p[:, qb:2 * qb] + p[:, 2 * qb:3 * qb]
        imp_t.append(_dot_exact01(ovl_ref[...].astype(BF16), p_sum))

    blk = ri
    tq_lane = q0 + ci
    cur = lax.shift_right_logical(tq_lane, 6)
    forced = (blk == 0) | (blk == cur) | (blk == cur - 1)
    visible = blk * SEL_BLOCK <= tq_lane
    sel = []
    for g in range(N_NSA_KV):
        score = jnp.where(forced, FORCE_SCORE, jnp.where(visible, imp_t[g], -1.0))
        slabs = [score[8 * v:8 * v + 8, :] for v in range(n_sel // 8)]
        cnts = [jnp.zeros((8, LANES), F32) for _ in slabs]
        row8 = _iota((8, LANES), 0)
        for i in range(n_sel):
            si = jnp.broadcast_to(score[i:i + 1, :], (8, LANES))
            for v, slab in enumerate(slabs):
                if 8 * v > i:
                    beats = si >= slab
                elif 8 * v + 7 < i:
                    beats = si > slab
                else:
                    beats = (si > slab) | ((si == slab) & (row8 > i - 8 * v))
                cnts[v] = cnts[v] + jnp.where(beats, 1.0, 0.0)
        picked_rows = [jnp.where(c < float(SEL_TOPK), 1.0, 0.0) for c in cnts]
        if n_sel < LANES:
            picked_rows.append(jnp.zeros((LANES - n_sel, LANES), F32))
        sel.append(jnp.concatenate(picked_rows, axis=0).astype(BF16))

    groups = range(N_NSA_KV)

    def softmax_update(raw, msks, vt_refs, tile_ids, m_run):
        m_out = []
        for g in groups:
            ss = [per_head(lambda x, mk=mk: jnp.where(mk, x, NEG), s) for s, mk in zip(raw[g], msks[g])]
            m_new = m_run[g]
            for s in ss:
                m_new = jnp.maximum(m_new, jnp.max(s, axis=0, keepdims=True))
            acc = acc_ref[g] * jnp.exp(m_run[g] - m_new)
            ps = [per_head(lambda x, mk=mk: jnp.where(mk, x, 0.0), jnp.exp(s - m_new)).astype(BF16)
                  for s, mk in zip(ss, msks[g])]
            for t, p in zip(tile_ids, ps):
                acc = acc + jnp.dot(vt_refs[g][t], p, preferred_element_type=F32)
            acc_ref[g] = acc
            m_out.append(m_new)
        return tuple(m_out)

    def normalised():
        outs = []
        for g in range(N_NSA_KV):
            acc = acc_ref[g]
            denom = acc[HEAD_DIM:HEAD_DIM + 1, :] if g == 0 else acc[0:1, :]
            outs.append(acc / jnp.maximum(denom, 1e-30))
        return outs

    def reset():
        for g in range(N_NSA_KV):
            acc_ref[g] = jnp.zeros((LANES, cols3), F32)
        return tuple(jnp.full((1, cols3), NEG, F32) for _ in range(N_NSA_KV))

    def sel_tiles(j):
        return [NSA_TILES * j + a for a in range(NSA_TILES)]

    buf_a, buf_b = (raw_a, pk_a), (raw_b, pk_b)

    def sel_issue(j, buf, scores=True):
        raw_ref, pk_ref = buf
        for a, t in enumerate(sel_tiles(j)):
            kt = ks_ref[pl.ds(pl.multiple_of(t * LANES, LANES), LANES), :]
            expand = jnp.where(lax.shift_right_logical(ri, 6) + 2 * t == ci, 1.0, 0.0).astype(BF16)
            for g in groups:
                if scores:
                    raw_ref[g, a] = jnp.dot(kt, q_grp[g], preferred_element_type=F32)
                pk_ref[g, a] = jnp.dot(expand, sel[g], preferred_element_type=F32)

    def sel_consume(j, buf, m_run, diagonal):
        raw_ref, pk_ref = buf
        tiles = sel_tiles(j)
        raw = [[raw_ref[g, a] for a in range(NSA_TILES)] for g in groups]
        msks = []
        for g in groups:
            row = []
            for a, t in enumerate(tiles):
                mk = pk_ref[g, a] > 0.5
                if diagonal:
                    mk = mk & (t * LANES + ri <= tq_lane)
                row.append(mk)
            msks.append(row)
        return softmax_update(raw, msks, (vst0_ref, vst1_ref), tiles, m_run)

    last = qi // NSA_TILES
    sel_issue(0, buf_a, scores=False)

    def sel_body(i, m_run):
        sel_issue(2 * i + 1, buf_b)
        m_run = sel_consume(2 * i, buf_a, m_run, False)
        sel_issue(2 * i + 2, buf_a)
        return sel_consume(2 * i + 1, buf_b, m_run, False)

    m_sel = lax.fori_loop(0, last // 2, sel_body, reset())

    def odd_tail(m_run):
        sel_issue(last, buf_b)
        m_run = sel_consume(last - 1, buf_a, m_run, False)
        sel_consume(last, buf_b, m_run, True)
        return 0

    def even_tail(m_run):
        sel_consume(last, buf_a, m_run, True)
        return 0

    lax.cond((last & 1) == 1, odd_tail, even_tail, m_sel)
    o_s = normalised()
    m0 = reset()
    msk_w = []
    for t, tc in zip(w_tiles, w_clamped):
        diff = tq_lane - (jnp.where(t >= 0, tc * LANES, 1 << 30) + ri)
        msk_w.append((diff >= 0) & (diff < WINDOW))
    softmax_update([[raw_w[g, a] for a in range(len(w_tiles))] for g in groups], [msk_w] * N_NSA_KV,
                   (vwt0_ref, vwt1_ref), w_clamped, m0)
    o_w = normalised()

    gate_t = jax.nn.sigmoid(s_ref[...]).T
    nw_col = nw_ref[...]
    for r in range(NSA_REP):
        cs = slice(r * qb, (r + 1) * qb)
        comb = []
        for g in range(N_NSA_KV):
            h = g * NSA_REP + r
            gc, gs, gw = (gate_t[SMALL_GATE + br * N_NSA + h: SMALL_GATE + br * N_NSA + h + 1, :] for br in range(3))
            comb.append(gc * o_c[g][:, cs] + gs * o_s[g][:, cs] + gw * o_w[g][:, cs])
        o_t = jnp.where(top, comb[0], comb[1])
        sq = o_t * o_t
        ms0 = jnp.sum(jnp.where(top, sq, 0.0), axis=0, keepdims=True)
        ms1 = jnp.sum(jnp.where(top, 0.0, sq), axis=0, keepdims=True)
        o_t = o_t * lax.rsqrt(jnp.where(top, ms0, ms1) * (1.0 / HEAD_DIM) + EPS) * nw_col
        o_ref[:, r * LANES:(r + 1) * LANES] = o_t.T


def _nsa(q_roped, proj, kcmp, vcmp_t, ks, vst0, vst1, kw, vwt0, vwt1, ovl_t, norm_col, bsz, t_len):
    n = proj.shape[0]
    nq = t_len // Q_BLOCK
    n_cmp = t_len // CMP_STRIDE
    per_b = lambda rows: pl.BlockSpec((rows, LANES), lambda b, i: (b, 0))
    tiles = pl.BlockSpec((t_len // LANES, LANES, LANES), lambda b, i: (b, 0, 0))
    return pl.pallas_call(
        _nsa_kernel,
        grid=(bsz, nq),
        in_specs=[pl.BlockSpec((Q_BLOCK, 3 * LANES), lambda b, i: (b * nq + i, 0)),
                  pl.BlockSpec((Q_BLOCK, LANES), lambda b, i: (b * nq + i, CB_SMALL)),
                  per_b(n_cmp), pl.BlockSpec((LANES, n_cmp), lambda b, i: (b, 0)),
                  per_b(t_len), tiles, tiles, per_b(t_len), tiles, tiles,
                  pl.BlockSpec(ovl_t.shape, lambda b, i: (0, 0)),
                  pl.BlockSpec((LANES, 1), lambda b, i: (0, 0))],
        out_specs=pl.BlockSpec((Q_BLOCK, 3 * LANES), lambda b, i: (b * nq + i, 0)),
        out_shape=jax.ShapeDtypeStruct((n, D_NSA), F32),
        scratch_shapes=[pltpu.VMEM((N_NSA_KV, LANES, NSA_REP * Q_BLOCK), F32),
                        pltpu.VMEM((N_NSA_KV, NSA_TILES, LANES, NSA_REP * Q_BLOCK), F32),
                        pltpu.VMEM((N_NSA_KV, NSA_TILES, LANES, NSA_REP * Q_BLOCK), F32),
                        pltpu.VMEM((N_NSA_KV, NSA_TILES, LANES, Q_BLOCK), F32),
                        pltpu.VMEM((N_NSA_KV, NSA_TILES, LANES, Q_BLOCK), F32),
                        pltpu.VMEM((N_NSA_KV, WINDOW // LANES + 1, LANES, NSA_REP * Q_BLOCK), F32)],
        compiler_params=_cparams(2, 48),
        name="nsa",
    )(q_roped, proj, kcmp, vcmp_t, ks, vst0, vst1, kw, vwt0, vwt1, ovl_t, norm_col)


def _sb_kernel(q_ref, k_ref, v_ref, nw_ref, o_ref):
    qi = pl.program_id(1)
    qb = Q_BLOCK
    n_pairs = N_SB // 2
    ri = _iota((qb, LANES), 0)
    ci = _iota((qb, LANES), 1)
    lo = ci < HEAD_DIM
    upper_ones = jnp.concatenate([jnp.where(ri > ci, 1.0, 0.0), jnp.ones((qb, LANES), F32)], axis=1).astype(BF16)
    diag_strict = ci < ri
    zero_bf = jnp.zeros((qb, LANES), BF16)
    q_heads = []
    for p in range(n_pairs):
        q = q_ref[:, p * LANES:(p + 1) * LANES] * BF16(HEAD_DIM ** -0.5)
        q_heads += [jnp.where(lo if j == 0 else jnp.logical_not(lo), q, zero_bf) for j in range(2)]
    n_heads = len(q_heads)

    def tile_pair(jt, carries, accs, first_on_diag, n_slots=2):
        units = []
        vts = {}
        for slot in range(n_slots):
            t = jt - slot
            ok = t >= 0
            k0 = pl.multiple_of(jnp.maximum(t, 0) * LANES, LANES)
            keep = jnp.where(ok, 1.0, 0.0)
            for p in range(n_pairs):
                kt = k_ref[pl.ds(k0, LANES), p * LANES:(p + 1) * LANES]
                vts[slot, p] = v_ref[pl.ds(k0, LANES), p * LANES:(p + 1) * LANES]
                for j in range(2):
                    h = 2 * p + j
                    units.append((h, slot, keep, lax.dot_general(q_heads[h], kt, NT_DIMS, preferred_element_type=F32)))
        log_1ms, sums = [], []
        for j, slot, keep, z in units:
            log_1m = -(jnp.maximum(z, 0.0) + jnp.log1p(jnp.exp(-jnp.abs(z)))) * keep
            if first_on_diag and slot == 0:
                log_1m = jnp.where(diag_strict, log_1m, 0.0)
            log_1ms.append(log_1m)
            sums.append(_dot_exact01_rhs(log_1m, upper_ones))
        carries, accs = list(carries), list(accs)
        weights = []
        for (j, slot, keep, z), log_1m, sm in zip(units, log_1ms, sums):
            a = jnp.exp(z + log_1m + sm[:, :LANES] + carries[j]) * keep
            if first_on_diag and slot == 0:
                a = jnp.where(diag_strict, a, 0.0)
            weights.append(a.astype(BF16))
            carries[j] = carries[j] + sm[:, LANES:]
        for (j, slot, _, _), a in zip(units, weights):
            accs[j] = accs[j] + jnp.dot(a, vts[slot, j // 2], preferred_element_type=F32)
        return tuple(carries), tuple(accs)

    zeros = (jnp.zeros((qb, LANES), F32),) * n_heads
    carries, accs = tile_pair(qi, zeros, zeros, True, n_slots=SB_FIRST_TILES)

    def cond(c):
        jt, carries_, _ = c
        worst = carries_[0]
        for c_h in carries_[1:]:
            worst = jnp.maximum(worst, c_h)
        return jnp.logical_and(jt >= 0, jnp.max(worst) > EXP_UNDERFLOW)

    def body(c):
        jt, carries_, accs_ = c
        carries_, accs_ = tile_pair(jt, carries_, accs_, False)
        return jt - 2, carries_, accs_

    _, _, outs = lax.while_loop(cond, body, (qi - SB_FIRST_TILES, carries, accs))
    for p in range(n_pairs):
        o_ref[:, p * LANES:(p + 1) * LANES] = _head_rmsnorm(jnp.where(lo, outs[2 * p], outs[2 * p + 1]), nw_ref[...], lo)


def _sb(sq, sk, sv, norm_row, bsz, t_len):
    n = sq.shape[0]
    nq = t_len // Q_BLOCK
    return pl.pallas_call(
        _sb_kernel,
        grid=(bsz, nq),
        in_specs=[pl.BlockSpec((Q_BLOCK, D_SB), lambda b, i: (b * nq + i, 0)),
                  pl.BlockSpec((t_len, D_SB), lambda b, i: (b, 0)),
                  pl.BlockSpec((t_len, D_SB), lambda b, i: (b, 0)),
                  pl.BlockSpec((1, LANES), lambda b, i: (0, 0))],
        out_specs=pl.BlockSpec((Q_BLOCK, D_SB), lambda b, i: (b * nq + i, 0)),
        out_shape=jax.ShapeDtypeStruct((n, D_SB), F32),
        compiler_params=_cparams(2, 32),
        name="sb",
    )(sq, sk, sv, norm_row)


def _mix_kernel(h_ref, og_ref, on_ref, os_ref, wg_ref, wn_ref, ws_ref, o_ref):
    acc = h_ref[...]
    acc = acc + jnp.dot(og_ref[...].astype(BF16), wg_ref[...], preferred_element_type=F32)
    acc = acc + jnp.dot(on_ref[...].astype(BF16), wn_ref[...], preferred_element_type=F32)
    acc = acc + jnp.dot(os_ref[...].astype(BF16), ws_ref[...], preferred_element_type=F32)
    o_ref[...] = acc


def _mix(h, o_gdn, o_nsa, o_sb, wg, wn, ws, tm=512):
    n, d = h.shape
    rows = lambda a: pl.BlockSpec((tm, a.shape[1]), lambda i: (i, 0))
    full = lambda a: pl.BlockSpec(a.shape, lambda i: (0, 0))
    return pl.pallas_call(
        _mix_kernel,
        grid=(n // tm,),
        in_specs=[rows(h), rows(o_gdn), rows(o_nsa), rows(o_sb), full(wg), full(wn), full(ws)],
        out_specs=rows(h),
        out_shape=jax.ShapeDtypeStruct((n, d), F32),
        compiler_params=_cparams(1, 40),
        name="mix",
    )(h, o_gdn, o_nsa, o_sb, wg, wn, ws)


def _ffn_kernel(h_ref, halo_ref, g_ref, wg_ref, wu_ref, cg_ref, cu_ref, wd_ref, o_ref, xn_ref, acc_ref, *, t_len):
    i = pl.program_id(0)
    j = pl.program_id(1)
    tm = h_ref.shape[0]
    pad = halo_ref.shape[0]

    @pl.when(j == 0)
    def _():
        halo_keep = jnp.where((i * tm) % t_len == 0, 0.0, 1.0)
        xn_ref[pl.ds(0, pad), :] = (_rmsnorm_rows(halo_ref[...], g_ref[...]) * halo_keep).astype(BF16)
        xn_ref[pl.ds(pad, tm), :] = _rmsnorm_rows(h_ref[...], g_ref[...]).astype(BF16)
        acc_ref[...] = jnp.zeros(acc_ref.shape, F32)

    xn = xn_ref[...]

    def conv(w_ref, c_ref):
        u = jnp.dot(xn, w_ref[...], preferred_element_type=F32)
        out = None
        for s in range(FFN_CONV):
            term = u[pad - (FFN_CONV - 1) + s: pad - (FFN_CONV - 1) + s + tm, :] * c_ref[s:s + 1, :]
            out = term if out is None else out + term
        return out

    act = _silu(conv(wg_ref, cg_ref)) * conv(wu_ref, cu_ref)
    acc_ref[...] += jnp.dot(act.astype(BF16), wd_ref[...], preferred_element_type=F32)

    @pl.when(j == pl.num_programs(1) - 1)
    def _():
        o_ref[...] = h_ref[...] + acc_ref[...]


def _ffn(h, gain, w_up, conv_w, w_down, t_len, tm=512, tf=1408):
    n, d = h.shape
    d_ff = w_down.shape[0]
    nf = d_ff // tf
    pad = 16
    return pl.pallas_call(
        functools.partial(_ffn_kernel, t_len=t_len),
        grid=(n // tm, nf),
        in_specs=[pl.BlockSpec((tm, d), lambda i, j: (i, 0)),
                  pl.BlockSpec((pad, d), lambda i, j: (jnp.maximum(i * (tm // pad) - 1, 0), 0)),
                  pl.BlockSpec((1, d), lambda i, j: (0, 0)),
                  pl.BlockSpec((d, tf), lambda i, j: (0, j)),
                  pl.BlockSpec((d, tf), lambda i, j: (0, nf + j)),
                  pl.BlockSpec((FFN_CONV, tf), lambda i, j: (0, j)),
                  pl.BlockSpec((FFN_CONV, tf), lambda i, j: (0, nf + j)),
                  pl.BlockSpec((tf, d), lambda i, j: (j, 0))],
        out_specs=pl.BlockSpec((tm, d), lambda i, j: (i, 0)),
        out_shape=jax.ShapeDtypeStruct((n, d), F32),
        scratch_shapes=[pltpu.VMEM((tm + pad, d), BF16), pltpu.VMEM((tm, d), F32)],
        compiler_params=_cparams(2, 56),
        name="ffn",
    )(h, h, gain, w_up, w_up, conv_w, conv_w, w_down)


def _ple_kernel(h_ref, p_ref, g_ref, wg_ref, wp_ref, pn_ref, fin_ref, o_ref, *, final_norm):
    h = h_ref[...]
    gate = jax.nn.sigmoid(jnp.dot(_rmsnorm_rows(h, g_ref[...]).astype(BF16), wg_ref[...], preferred_element_type=F32))
    emb = jnp.dot(p_ref[...].astype(BF16), wp_ref[...], preferred_element_type=F32)
    out = h + gate * _rmsnorm_rows(emb, pn_ref[...])
    if final_norm:
        out = _rmsnorm_rows(out, fin_ref[...])
    o_ref[...] = out


def _ple(h, p_all, layer, gain, w_gate, w_ple, ple_gain, fin_gain, final_norm, tm=512):
    n, d = h.shape
    rows = lambda a: pl.BlockSpec((tm, a.shape[1]), lambda i: (i, 0))
    full = lambda a: pl.BlockSpec(a.shape, lambda i: (0, 0))
    p_spec = pl.BlockSpec((None, tm, p_all.shape[2]), lambda i: (layer, i, 0))
    return pl.pallas_call(
        functools.partial(_ple_kernel, final_norm=final_norm),
        grid=(n // tm,),
        in_specs=[rows(h), p_spec, full(gain), full(w_gate), full(w_ple), full(ple_gain), full(fin_gain)],
        out_specs=rows(h),
        out_shape=jax.ShapeDtypeStruct((n, d), F32),
        compiler_params=_cparams(1, 40),
        name="ple",
    )(h, p_all, gain, w_gate, w_ple, ple_gain, fin_gain)


def _overlap_t(t_len):
    n_cmp = t_len // CMP_STRIDE
    n_sel = t_len // SEL_BLOCK
    c0 = np.arange(n_cmp) * CMP_STRIDE
    s0 = np.arange(n_sel) * SEL_BLOCK
    ov = np.clip(np.minimum(c0[None, :] + CMP_LEN, s0[:, None] + SEL_BLOCK) - np.maximum(c0[None, :], s0[:, None]), 0, None)
    out = np.zeros((LANES, n_cmp), np.float32)
    out[:n_sel] = ov.astype(np.float32) / CMP_LEN
    return jnp.asarray(out)


def _tile_row(v, reps):
    return jnp.tile(v.astype(F32), reps).reshape(1, -1)


def kernel(x, p, positions, ln_mix, w_in, gdn_conv, gdn_a_log, gdn_dt_bias, gdn_norm, nsa_pe_k, nsa_pe_v, nsa_cmp_k_w1, nsa_cmp_k_w2, nsa_cmp_v_w1, nsa_cmp_v_w2, nsa_norm, sb_norm, w_out, ln_ffn, w_up, ffn_conv, w_down, ln_ple, w_ple_gate, w_ple, ple_norm, ln_final):
    bsz, t_len, d_model = x.shape
    depth = w_in.shape[0]
    n = bsz * t_len
    assert t_len % (NSA_TILES * LANES) == 0 and 2 < t_len // SEL_BLOCK <= LANES

    cols = _inproj_columns()
    mix_rows = _mix_rows()
    ovl_t = _overlap_t(t_len)

    half = HEAD_DIM // 2
    inv = ROPE_THETA ** (-jnp.arange(half, dtype=F32) / half)
    inv_row = jnp.tile(inv, LANES // half).reshape(1, LANES)
    sgn_row = jnp.asarray(np.where((np.arange(LANES) % HEAD_DIM) < half, -1.0, 1.0).astype(np.float32)).reshape(1, LANES)
    cos_t, sin_t = _rope_tables(positions.reshape(n, 1), inv_row, sgn_row)

    pad_lanes = lambda v: jnp.pad(v.astype(F32), (0, LANES - v.shape[0])).reshape(1, LANES)
    h = x.reshape(n, d_model)
    for i in range(depth):
        w_in_p = _permute_static(w_in[i].astype(BF16), cols, axis=1)
        proj = _inproj(h, ln_mix[i].reshape(1, d_model), w_in_p)

        o_gdn = _gdn(proj, gdn_conv[i], pad_lanes(gdn_a_log[i]), pad_lanes(gdn_dt_bias[i]),
                     _tile_row(gdn_norm[i], 2), bsz, t_len)

        q_r, kc_r, ks_r, kw_r, vst0, vst1, vwt0, vwt1, sq_b, sk_b, sv_b = _prep(proj, cos_t, sin_t)
        w1dup = lambda w1: jnp.tile(w1.reshape(CMP_LEN, HEAD_DIM, -1), (1, 2, 1)).astype(BF16)
        w2pad = lambda w2: jnp.stack([jnp.pad(w2, ((0, 0), (0, HEAD_DIM))), jnp.pad(w2, ((0, 0), (HEAD_DIM, 0)))]).astype(BF16)
        kcmp, vcmp = _compress(kc_r, proj, jnp.tile(nsa_pe_k[i], (1, 2)), jnp.tile(nsa_pe_v[i], (1, 2)),
                               w1dup(nsa_cmp_k_w1[i]), w1dup(nsa_cmp_v_w1[i]),
                               w2pad(nsa_cmp_k_w2[i]), w2pad(nsa_cmp_v_w2[i]), bsz, t_len)
        o_nsa = _nsa(q_r, proj, kcmp, vcmp, ks_r, vst0, vst1, kw_r, vwt0, vwt1, ovl_t,
                     _tile_row(nsa_norm[i], 2).reshape(LANES, 1), bsz, t_len)
        o_sb = _sb(sq_b, sk_b, sv_b, _tile_row(sb_norm[i], 2), bsz, t_len)

        w_o = _permute_static(w_out[i].astype(BF16), mix_rows, axis=0)
        h = _mix(h, o_gdn, o_nsa, o_sb, w_o[:D_GDN], w_o[D_GDN:D_GDN + D_NSA], w_o[D_GDN + D_NSA:])

        h = _ffn(h, ln_ffn[i].reshape(1, d_model), w_up[i].astype(BF16), ffn_conv[i], w_down[i].astype(BF16), t_len)

        h = _ple(h, p.reshape(depth, n, -1), i, ln_ple[i].reshape(1, d_model), w_ple_gate[i].astype(BF16),
                 w_ple[i].astype(BF16), ple_norm[i].reshape(1, d_model), ln_final.reshape(1, d_model),
                 final_norm=(i == depth - 1))
    return h.reshape(bsz, t_len, d_model)
```

```python
import functools

import jax
import jax.numpy as jnp
import numpy as np
from jax import lax
from jax.experimental import pallas as pl
from jax.experimental.pallas import tpu as pltpu

F32 = jnp.float32
BF16 = jnp.bfloat16
HIGHEST = lax.Precision.HIGHEST

LANES = 128
HEAD_DIM = 64
N_GDN = 6
N_NSA = 6
N_NSA_KV = 2
NSA_REP = N_NSA // N_NSA_KV
N_SB = 4
D_GDN = N_GDN * HEAD_DIM
D_NSA = N_NSA * HEAD_DIM
D_SB = N_SB * HEAD_DIM
GDN_CONV = 4
GDN_BLOCK = 128
GDN_SUBBLOCKS = 4
CMP_LEN = 32
CMP_STRIDE = 16
SEL_BLOCK = 64
SEL_TOPK = 16
WINDOW = 512
Q_BLOCK = 128
NSA_TILES = 4
SB_FIRST_TILES = 4
FFN_CONV = 3
FFN_TF = 1408
ROPE_THETA = 10000.0
EPS = 1e-6
NEG = -1e30
FORCE_SCORE = 1e4
EXP_UNDERFLOW = -104.0

CB_GQ, CB_GK, CB_GV, CB_GZ = 0, 3, 6, 9
CB_NQ = 12
CB_SMALL = 15
CB_KC, CB_VC, CB_KS, CB_VS, CB_KW, CB_VW = 16, 17, 18, 19, 20, 21
CB_SQ, CB_SK, CB_SV = 22, 24, 26
N_CB = 28
NP_IN = N_CB * LANES
SMALL_A, SMALL_B, SMALL_GATE = 0, 6, 12

NN_DIMS = (((1,), (0,)), ((), ()))
NT_DIMS = (((1,), (1,)), ((), ()))
TN_DIMS = (((0,), (0,)), ((), ()))


def _cparams(n_axes, vmem_mb):
    return pltpu.CompilerParams(dimension_semantics=("arbitrary",) * n_axes,
                                vmem_limit_bytes=vmem_mb * 1024 * 1024)


def _inproj_columns():
    d_gdn, d_nsa, d_kv, d_sb = D_GDN, D_NSA, N_NSA_KV * HEAD_DIM, D_SB
    o = {}
    off = 0
    for name, size in (("gq", d_gdn), ("gk", d_gdn), ("gv", d_gdn), ("gz", d_gdn), ("ga", N_GDN), ("gb", N_GDN),
                       ("nq", d_nsa), ("kc", d_kv), ("vc", d_kv), ("ks", d_kv), ("vs", d_kv), ("kw", d_kv),
                       ("vw", d_kv), ("ng", 3 * N_NSA), ("sq", d_sb), ("sk", d_sb), ("sv", d_sb)):
        o[name] = off
        off += size
    cols = -np.ones((NP_IN,), np.int64)

    def put(cb, lane, src, size):
        cols[cb * LANES + lane: cb * LANES + lane + size] = np.arange(src, src + size)

    put(CB_GQ, 0, o["gq"], d_gdn)
    put(CB_GK, 0, o["gk"], d_gdn)
    put(CB_GV, 0, o["gv"], d_gdn)
    put(CB_GZ, 0, o["gz"], d_gdn)
    for r in range(NSA_REP):
        for g in range(N_NSA_KV):
            put(CB_NQ + r, g * HEAD_DIM, o["nq"] + (g * NSA_REP + r) * HEAD_DIM, HEAD_DIM)
    put(CB_SMALL, SMALL_A, o["ga"], N_GDN)
    put(CB_SMALL, SMALL_B, o["gb"], N_GDN)
    put(CB_SMALL, SMALL_GATE, o["ng"], 3 * N_NSA)
    for cb, nm in ((CB_KC, "kc"), (CB_VC, "vc"), (CB_KS, "ks"), (CB_VS, "vs"), (CB_KW, "kw"), (CB_VW, "vw")):
        put(cb, 0, o[nm], d_kv)
    put(CB_SQ, 0, o["sq"], d_sb)
    put(CB_SK, 0, o["sk"], d_sb)
    put(CB_SV, 0, o["sv"], d_sb)
    return cols


def _mix_rows():
    rows = list(range(D_GDN))
    for c in range(D_NSA):
        r, half, d = c // LANES, (c % LANES) // HEAD_DIM, c % HEAD_DIM
        rows.append(D_GDN + (half * NSA_REP + r) * HEAD_DIM + d)
    rows += list(range(D_GDN + D_NSA, D_GDN + D_NSA + D_SB))
    return np.asarray(rows, np.int64)


def _dot_bf16(a, b, dims=NN_DIMS):
    return lax.dot_general(a.astype(BF16), b.astype(BF16), dims, preferred_element_type=F32)


def _dot_exact01(m01_bf16, x):
    h1 = x.astype(BF16)
    r1 = x - h1.astype(F32)
    h2 = r1.astype(BF16)
    h3 = (r1 - h2.astype(F32)).astype(BF16)
    d = lambda y: jnp.dot(m01_bf16, y, preferred_element_type=F32)
    return (d(h3) + d(h2)) + d(h1)


def _dot_exact01_rhs(x, m01_bf16):
    h1 = x.astype(BF16)
    r1 = x - h1.astype(F32)
    h2 = r1.astype(BF16)
    h3 = (r1 - h2.astype(F32)).astype(BF16)
    d = lambda y: jnp.dot(y, m01_bf16, preferred_element_type=F32)
    return (d(h3) + d(h2)) + d(h1)


def _permute_static(w, index, axis):
    index = np.asarray(index)
    pieces, start = [], 0
    while start < len(index):
        stop = start + 1
        if index[start] < 0:
            while stop < len(index) and index[stop] < 0:
                stop += 1
            shape = list(w.shape)
            shape[axis] = stop - start
            pieces.append(jnp.zeros(shape, w.dtype))
        else:
            while stop < len(index) and index[stop] == index[stop - 1] + 1:
                stop += 1
            pieces.append(lax.slice_in_dim(w, int(index[start]), int(index[stop - 1]) + 1, axis=axis))
        start = stop
    return jnp.concatenate(pieces, axis=axis)


def _iota(shape, dim):
    return lax.broadcasted_iota(jnp.int32, shape, dim)


def _silu(x):
    return x * jax.nn.sigmoid(x)


def _head_sum(x, lo):
    s0 = jnp.sum(jnp.where(lo, x, 0.0), axis=1, keepdims=True)
    s1 = jnp.sum(jnp.where(lo, 0.0, x), axis=1, keepdims=True)
    return jnp.where(lo, s0, s1)


def _head_rmsnorm(x, w_row, lo):
    return x * lax.rsqrt(_head_sum(x * x, lo) * (1.0 / HEAD_DIM) + EPS) * w_row


def _rmsnorm_rows(x, w_row):
    return x * lax.rsqrt(jnp.mean(x * x, axis=-1, keepdims=True) + EPS) * w_row


def _inproj_kernel(x_ref, g_ref, w_ref, o_ref, xn_ref):
    @pl.when(pl.program_id(1) == 0)
    def _():
        xn_ref[...] = _rmsnorm_rows(x_ref[...], g_ref[...]).astype(BF16)

    o_ref[...] = jnp.dot(xn_ref[...], w_ref[...], preferred_element_type=F32)


def _inproj(h, gain, w_bf16, tm=512, tn=NP_IN):
    n, d = h.shape
    return pl.pallas_call(
        _inproj_kernel,
        grid=(n // tm, NP_IN // tn),
        in_specs=[pl.BlockSpec((tm, d), lambda i, j: (i, 0)),
                  pl.BlockSpec((1, d), lambda i, j: (0, 0)),
                  pl.BlockSpec((d, tn), lambda i, j: (0, j))],
        out_specs=pl.BlockSpec((tm, tn), lambda i, j: (i, j)),
        out_shape=jax.ShapeDtypeStruct((n, NP_IN), F32),
        scratch_shapes=[pltpu.VMEM((tm, d), BF16)],
        compiler_params=_cparams(2, 48),
        name="inproj",
    )(h, gain, w_bf16)


def _gdn_kernel(q_ref, k_ref, v_ref, z_ref, s_ref, qh_ref, kh_ref, vh_ref, cq_ref, ck_ref, cv_ref, alog_ref, dt_ref,
                nw_ref, o_ref, qp, kp, vp, st_ref):
    ti = pl.program_id(1)
    t_blk = q_ref.shape[0]
    blk = GDN_BLOCK
    pad = qh_ref.shape[0]
    n_pairs = N_GDN // 2
    halo_keep = jnp.where(ti == 0, 0.0, 1.0)
    for src, halo, dst in ((q_ref, qh_ref, qp), (k_ref, kh_ref, kp), (v_ref, vh_ref, vp)):
        for hp in range(n_pairs):
            dst[hp, pl.ds(0, pad), :] = halo[:, hp * LANES:(hp + 1) * LANES] * halo_keep
            dst[hp, pl.ds(pad, t_blk), :] = src[:, hp * LANES:(hp + 1) * LANES]

    @pl.when(ti == 0)
    def _():
        st_ref[...] = jnp.zeros(st_ref.shape, F32)

    ri = _iota((blk, LANES), 0)
    ci = _iota((blk, LANES), 1)
    lo = ci < HEAD_DIM
    incl = ri >= ci
    strict = ri > ci
    tri_f = jnp.where(incl, 1.0, 0.0)
    blockdiag = (ri < HEAD_DIM) == lo
    eye_f = jnp.where(ri == ci, 1.0, 0.0)

    def same_block(log2_size):
        return lax.shift_right_logical(ri, log2_size) == lax.shift_right_logical(ci, log2_size)
    scale = HEAD_DIM ** -0.5
    nw = nw_ref[...]

    tri_bf = tri_f.astype(BF16)
    idot = _dot_bf16
    rdot = _dot_bf16

    pairs = range(n_pairs)
    n_sub = GDN_SUBBLOCKS
    units = [(b, hp) for b in range(n_sub) for hp in pairs]
    heads = [(u, j) for u in range(len(units)) for j in range(2)]

    def body(c, states):
        base = pl.multiple_of(c * (n_sub * blk), n_sub * blk)
        r0 = [base + b * blk for b in range(n_sub)]

        def conv(xp, w_ref, u):
            b, hp = units[u]
            acc = None
            for s in range(GDN_CONV):
                term = (xp[hp, pl.ds(r0[b] + pad - (GDN_CONV - 1) + s, blk), :]
                        * w_ref[s:s + 1, hp * LANES:(hp + 1) * LANES])
                acc = term if acc is None else acc + term
            return _silu(acc)

        sm = [s_ref[pl.ds(r0[b], blk), :] for b in range(n_sub)]
        gk, beta = [], []
        for u, j in heads:
            b, hp = units[u]
            h = 2 * hp + j
            a_h = sm[b][:, SMALL_A + h:SMALL_A + h + 1]
            b_h = sm[b][:, SMALL_B + h:SMALL_B + h + 1]
            gk.append(jnp.broadcast_to(-jnp.exp(alog_ref[:, h:h + 1]) * jax.nn.softplus(a_h + dt_ref[:, h:h + 1]),
                                       (blk, LANES)))
            beta.append(jax.nn.sigmoid(b_h))
        g_col = [_dot_exact01(tri_bf, g) for g in gk]

        qn, kn, kb, vb, g_pair, eg_pair, rhs = [], [], [], [], [], [], []
        for u in range(len(units)):
            q = conv(qp, cq_ref, u)
            k = conv(kp, ck_ref, u)
            v = conv(vp, cv_ref, u)
            qn.append(q * lax.rsqrt(_head_sum(q * q, lo) + EPS) * scale)
            kn.append(k * lax.rsqrt(_head_sum(k * k, lo) + EPS))
            g_pair.append(jnp.where(lo, g_col[2 * u], g_col[2 * u + 1]))
            eg_pair.append(jnp.exp(g_pair[u]))
            beta_pair = jnp.where(lo, beta[2 * u], beta[2 * u + 1])
            kb.append(kn[u] * beta_pair)
            vb.append(v * beta_pair)
            kbg_sw = pltpu.roll(kb[u] * eg_pair[u], HEAD_DIM, 1)
            rhs += [jnp.where(lo, vb[u], kbg_sw), jnp.where(lo, kbg_sw, vb[u])]

        mask = [lo if j == 0 else jnp.logical_not(lo) for _, j in heads]
        decay = [jnp.exp(jnp.where(incl, g - g.T, NEG)) for g in g_col]
        kk = [rdot(jnp.where(mask[i], kb[u], 0.0), kn[u], NT_DIMS) for i, (u, _) in enumerate(heads)]
        qk = [rdot(jnp.where(mask[i], qn[u], 0.0), kn[u], NT_DIMS) for i, (u, _) in enumerate(heads)]
        lower = [jnp.where(strict, a * d, 0.0) for a, d in zip(kk, decay)]
        attn = [jnp.where(incl, a * d, 0.0) for a, d in zip(qk, decay)]
        d1 = [jnp.where(same_block(3), a, 0.0) for a in lower]
        d2 = [idot(a, a) for a in d1]
        x = [eye_f - a for a in d1]
        x = [xi + idot(xi, a) for xi, a in zip(x, d2)]
        d4 = [idot(a, a) for a in d2]
        x = [xi + idot(xi, a) for xi, a in zip(x, d4)]
        for lg in range(4, 8):
            off_mask = same_block(lg) & jnp.logical_not(same_block(lg - 1))
            y = [idot(jnp.where(off_mask, a, 0.0), xi) for a, xi in zip(lower, x)]
            x = [xi - idot(xi, yi) for xi, yi in zip(x, y)]
        sol = [idot(xi, r) for xi, r in zip(x, rhs)]
        u_all = [jnp.where(lo, sol[2 * u], sol[2 * u + 1]) for u in range(len(units))]
        w_all = [pltpu.roll(jnp.where(lo, sol[2 * u + 1], sol[2 * u]), HEAD_DIM, 1) for u in range(len(units))]
        g_last = [g[blk - 1:blk, :] for g in g_pair]

        states = list(states)
        for b in range(n_sub):
            us = [b * n_pairs + hp for hp in pairs]
            v_new = [u_all[u] - rdot(w_all[u], states[hp]) for hp, u in zip(pairs, us)]
            inter = [rdot(qn[u] * eg_pair[u], states[hp]) for hp, u in zip(pairs, us)]
            intra = [rdot(attn[2 * u + j], v_new[hp]) for hp, u in zip(pairs, us) for j in range(2)]
            kv = [rdot(kn[u] * jnp.exp(g_last[u] - g_pair[u]), v_new[hp], TN_DIMS) for hp, u in zip(pairs, us)]
            for hp, u in zip(pairs, us):
                o = inter[hp] + jnp.where(lo, intra[2 * hp], intra[2 * hp + 1])
                cs = slice(hp * LANES, (hp + 1) * LANES)
                o_ref[pl.ds(r0[b], blk), cs] = _head_rmsnorm(o, nw, lo) * _silu(z_ref[pl.ds(r0[b], blk), cs])
                states[hp] = states[hp] * jnp.exp(g_last[u]) + jnp.where(blockdiag, kv[hp], 0.0)
        return tuple(states)

    states = lax.fori_loop(0, t_blk // (n_sub * blk), body, tuple(st_ref[hp] for hp in pairs))
    for hp in pairs:
        st_ref[hp] = states[hp]


def _gdn(proj, conv_w, a_log_row, dt_row, norm_row, bsz, t_len, t_blk=1024):
    n = proj.shape[0]
    nt = t_len // t_blk
    pad = 8
    w3 = 3 * LANES
    main = lambda cb: pl.BlockSpec((t_blk, w3), lambda b, t: (b * nt + t, cb // 3))
    halo = lambda cb: pl.BlockSpec((pad, w3), lambda b, t: (jnp.maximum((b * nt + t) * (t_blk // pad) - 1, 0), cb // 3))
    cw = lambda j: pl.BlockSpec((GDN_CONV, w3), lambda b, t: (0, j))
    row = pl.BlockSpec((1, LANES), lambda b, t: (0, 0))
    return pl.pallas_call(
        _gdn_kernel,
        grid=(bsz, nt),
        in_specs=[main(CB_GQ), main(CB_GK), main(CB_GV), main(CB_GZ),
                  pl.BlockSpec((t_blk, LANES), lambda b, t: (b * nt + t, CB_SMALL)),
                  halo(CB_GQ), halo(CB_GK), halo(CB_GV),
                  cw(0), cw(1), cw(2), row, row, row],
        out_specs=pl.BlockSpec((t_blk, w3), lambda b, t: (b * nt + t, 0)),
        out_shape=jax.ShapeDtypeStruct((n, D_GDN), F32),
        scratch_shapes=[pltpu.VMEM((N_GDN // 2, t_blk + pad, LANES), F32)] * 3 +[pltpu.VMEM((N_GDN // 2, LANES, LANES), F32)],
        compiler_params=_cparams(2, 48),
        name="gdn",
    )(proj, proj, proj, proj, proj, proj, proj, proj, conv_w, conv_w, conv_w, a_log_row, dt_row, norm_row)


def _rope_table_kernel(pos_ref, inv_ref, sgn_ref, cos_ref, sin_ref):
    ang = pos_ref[...].astype(F32) * inv_ref[...]
    cos_ref[...] = jnp.cos(ang)
    sin_ref[...] = jnp.sin(ang) * sgn_ref[...]


def _rope_tables(pos_col, inv_row, sgn_row, tm=1024):
    n = pos_col.shape[0]
    row = pl.BlockSpec((1, LANES), lambda i: (0, 0))
    out = pl.BlockSpec((tm, LANES), lambda i: (i, 0))
    return pl.pallas_call(
        _rope_table_kernel,
        grid=(n // tm,),
        in_specs=[pl.BlockSpec((tm, 1), lambda i: (i, 0)), row, row],
        out_specs=[out, out],
        out_shape=[jax.ShapeDtypeStruct((n, LANES), F32)] * 2,
        compiler_params=_cparams(1, 32),
        name="rope_tables",
    )(pos_col, inv_row, sgn_row)


def _prep_kernel(q_ref, kc_ref, ks_ref, vs_ref, kw_ref, vw_ref, sq_ref, sk_ref, sv_ref, cos_ref, sin_ref,
                 qo_ref, kco_ref, kso_ref, kwo_ref, vst0_ref, vst1_ref, vwt0_ref, vwt1_ref, sqo_ref, sko_ref, svo_ref):
    cos = cos_ref[...]
    sin = sin_ref[...]
    first = (_iota(cos.shape, 1) % HEAD_DIM) < (HEAD_DIM // 2)

    def rope(x):
        swapped = jnp.where(first, pltpu.roll(x, LANES - HEAD_DIM // 2, 1), pltpu.roll(x, HEAD_DIM // 2, 1))
        return x * cos + swapped * sin

    for r in range(NSA_REP):
        sl = slice(r * LANES, (r + 1) * LANES)
        qo_ref[:, sl] = rope(q_ref[:, sl]).astype(BF16)
    kco_ref[...] = rope(kc_ref[...])
    kso_ref[...] = rope(ks_ref[...]).astype(BF16)
    kwo_ref[...] = rope(kw_ref[...]).astype(BF16)
    top = _iota((LANES, LANES), 0) < HEAD_DIM
    for src, dst0, dst1 in ((vs_ref, vst0_ref, vst1_ref), (vw_ref, vwt0_ref, vwt1_ref)):
        for a in range(src.shape[0] // LANES):
            vt = src[a * LANES:(a + 1) * LANES, :].T
            dst0[a] = jnp.where(top, vt, 1.0).astype(BF16)
            dst1[a] = jnp.where(top, 1.0, vt).astype(BF16)
    sqo_ref[...] = sq_ref[...].astype(BF16)
    sko_ref[...] = sk_ref[...].astype(BF16)
    svo_ref[...] = sv_ref[...].astype(BF16)


def _prep(proj, cos_t, sin_t, tm=512):
    n = proj.shape[0]
    one = lambda cb: pl.BlockSpec((tm, LANES), lambda i: (i, cb))
    two = lambda cb: pl.BlockSpec((tm, 2 * LANES), lambda i: (i, cb // 2))
    tab = pl.BlockSpec((tm, LANES), lambda i: (i, 0))
    o1 = pl.BlockSpec((tm, LANES), lambda i: (i, 0))
    o2 = pl.BlockSpec((tm, 2 * LANES), lambda i: (i, 0))
    o3 = pl.BlockSpec((tm, 3 * LANES), lambda i: (i, 0))
    s1 = lambda dt: jax.ShapeDtypeStruct((n, LANES), dt)
    s2 = jax.ShapeDtypeStruct((n, 2 * LANES), BF16)
    ot = pl.BlockSpec((tm // LANES, LANES, LANES), lambda i: (i, 0, 0))
    st = jax.ShapeDtypeStruct((n // LANES, LANES, LANES), BF16)
    return pl.pallas_call(
        _prep_kernel,
        grid=(n // tm,),
        in_specs=[pl.BlockSpec((tm, 3 * LANES), lambda i: (i, CB_NQ // 3)),
                  one(CB_KC), one(CB_KS), one(CB_VS), one(CB_KW), one(CB_VW),
                  two(CB_SQ), two(CB_SK), two(CB_SV), tab, tab],
        out_specs=[o3, o1, o1, o1, ot, ot, ot, ot, o2, o2, o2],
        out_shape=[jax.ShapeDtypeStruct((n, 3 * LANES), BF16), s1(F32), s1(BF16), s1(BF16), st, st, st, st,
                   s2, s2, s2],
        compiler_params=_cparams(1, 40),
        name="prep",
    )(proj, proj, proj, proj, proj, proj, proj, proj, proj, cos_t, sin_t)


def _compress_kernel(k_ref, v_ref, pek_ref, pev_ref, w1k_ref, w1v_ref, w2k_ref, w2v_ref, ko_ref, vo_ref, xp):
    t_len = k_ref.shape[0]
    n_out = t_len // CMP_STRIDE
    lo = _iota((n_out, LANES), 1) < HEAD_DIM
    for src, pe_ref, w1_ref, w2_ref, out_ref in ((k_ref, pek_ref, w1k_ref, w2k_ref, ko_ref),
                                                 (v_ref, pev_ref, w1v_ref, w2v_ref, vo_ref)):
        xp[pl.ds(0, t_len), :] = src[...]
        xp[pl.ds(t_len, CMP_STRIDE), :] = jnp.zeros((CMP_STRIDE, LANES), F32)
        z0 = jnp.zeros((n_out, LANES), F32)
        z1 = jnp.zeros((n_out, LANES), F32)
        for l in range(CMP_LEN):
            xl = xp[pl.ds(l, n_out, stride=CMP_STRIDE), :] + pe_ref[l:l + 1, :]
            w = w1_ref[l]
            z0 = z0 + jnp.dot(jnp.where(lo, xl, 0.0).astype(BF16), w, preferred_element_type=F32)
            z1 = z1 + jnp.dot(jnp.where(lo, 0.0, xl).astype(BF16), w, preferred_element_type=F32)
        out = (jnp.dot(_silu(z0).astype(BF16), w2_ref[0], preferred_element_type=F32)
               + jnp.dot(_silu(z1).astype(BF16), w2_ref[1], preferred_element_type=F32))
        out_ref[...] = out if out_ref is ko_ref else out.T


def _compress(kc_roped, proj, pe_k2, pe_v2, w1k, w1v, w2k, w2v, bsz, t_len):
    n_out = t_len // CMP_STRIDE
    full = lambda a: pl.BlockSpec(a.shape, lambda b: (0,) * a.ndim)
    out = pl.BlockSpec((n_out, LANES), lambda b: (b, 0))
    return pl.pallas_call(
        _compress_kernel,
        grid=(bsz,),
        in_specs=[pl.BlockSpec((t_len, LANES), lambda b: (b, 0)),
                  pl.BlockSpec((t_len, LANES), lambda b: (b, CB_VC)),
                  full(pe_k2), full(pe_v2), full(w1k), full(w1v), full(w2k), full(w2v)],
        out_specs=[out, pl.BlockSpec((LANES, n_out), lambda b: (b, 0))],
        out_shape=[jax.ShapeDtypeStruct((bsz * n_out, LANES), F32), jax.ShapeDtypeStruct((bsz * LANES, n_out), F32)],
        scratch_shapes=[pltpu.VMEM((t_len + CMP_STRIDE, LANES), F32)],
        compiler_params=_cparams(1, 40),
        name="compress",
    )(kc_roped, proj, pe_k2, pe_v2, w1k, w1v, w2k, w2v)


def _nsa_kernel(q_ref, s_ref, kc_ref, vct_ref, ks_ref, vst0_ref, vst1_ref, kw_ref, vwt0_ref, vwt1_ref, ovl_ref,
                nw_ref, o_ref, acc_ref, raw_a, raw_b, pk_a, pk_b, raw_w):
    qi = pl.program_id(1)
    q0 = qi * Q_BLOCK
    qb = Q_BLOCK
    cols3 = NSA_REP * qb
    n_cmp = kc_ref.shape[0]
    n_sel = ks_ref.shape[0] // SEL_BLOCK

    ri = _iota((qb, LANES), 0)
    ci = _iota((qb, LANES), 1)
    top = ri < HEAD_DIM
    tq_lane = q0 + ci

    def per_head(fn, x):
        return jnp.concatenate([fn(x[:, r * qb:(r + 1) * qb]) for r in range(NSA_REP)], axis=1)

    q_t = [(q_ref[:, r * LANES:(r + 1) * LANES].astype(F32) * (HEAD_DIM ** -0.5)).T for r in range(NSA_REP)]
    q_grp = [jnp.concatenate([jnp.where(top if g == 0 else jnp.logical_not(top), t, 0.0) for t in q_t],
                             axis=1).astype(BF16) for g in range(N_NSA_KV)]

    w_tiles = [qi - WINDOW // LANES + a for a in range(WINDOW // LANES + 1)]
    w_clamped = [jnp.maximum(t, 0) for t in w_tiles]
    for a, t in enumerate(w_clamped):
        kt = kw_ref[pl.ds(pl.multiple_of(t * LANES, LANES), LANES), :]
        for g in range(N_NSA_KV):
            raw_w[g, a] = jnp.dot(kt, q_grp[g], preferred_element_type=F32)
    for a in range(NSA_TILES):
        kt = ks_ref[pl.ds(a * LANES, LANES), :]
        for g in range(N_NSA_KV):
            raw_a[g, a] = jnp.dot(kt, q_grp[g], preferred_element_type=F32)

    kc = kc_ref[...].astype(BF16)
    vct = vct_ref[...].astype(BF16)
    m_c = (_iota((n_cmp, cols3), 0) * CMP_STRIDE + (CMP_LEN - 1)
           <= q0 + (_iota((n_cmp, cols3), 1) & (qb - 1)))
    o_c, imp_t = [], []
    for g in range(N_NSA_KV):
        s = jnp.dot(kc, q_grp[g], preferred_element_type=F32)
        s = jnp.where(m_c, s, NEG)
        e = jnp.where(m_c, jnp.exp(s - jnp.max(s, axis=0, keepdims=True)), 0.0)
        p = e / jnp.maximum(jnp.sum(e, axis=0, keepdims=True), 1e-30)
        o_c.append(jnp.dot(vct, p.astype(BF16), preferred_element_type=F32))
        p_sum = p[:, 0:qb] + p[:, qb:2 * qb] + p[:, 2 * qb:3 * qb]
        imp_t.append(_dot_exact01(ovl_ref[...].astype(BF16), p_sum))

    blk = ri
    tq_lane = q0 + ci
    cur = lax.shift_right_logical(tq_lane, 6)
    forced = (blk == 0) | (blk == cur) | (blk == cur - 1)
    visible = blk * SEL_BLOCK <= tq_lane
    sel = []
    for g in range(N_NSA_KV):
        score = jnp.where(forced, FORCE_SCORE, jnp.where(visible, imp_t[g], -1.0))
        slabs = [score[8 * v:8 * v + 8, :] for v in range(n_sel // 8)]
        cnts = [jnp.zeros((8, LANES), F32) for _ in slabs]
        row8 = _iota((8, LANES), 0)
        for i in range(n_sel):
            si = jnp.broadcast_to(score[i:i + 1, :], (8, LANES))
            for v, slab in enumerate(slabs):
                if 8 * v > i:
                    beats = si >= slab
                elif 8 * v + 7 < i:
                    beats = si > slab
                else:
                    beats = (si > slab) | ((si == slab) & (row8 > i - 8 * v))
                cnts[v] = cnts[v] + jnp.where(beats, 1.0, 0.0)
        picked_rows = [jnp.where(c < float(SEL_TOPK), 1.0, 0.0) for c in cnts]
        if n_sel < LANES:
            picked_rows.append(jnp.zeros((LANES - n_sel, LANES), F32))
        sel.append(jnp.concatenate(picked_rows, axis=0).astype(BF16))

    groups = range(N_NSA_KV)

    def softmax_update(raw, msks, vt_refs, tile_ids, m_run):
        m_out = []
        for g in groups:
            ss = [per_head(lambda x, mk=mk: jnp.where(mk, x, NEG), s) for s, mk in zip(raw[g], msks[g])]
            m_new = m_run[g]
            for s in ss:
                m_new = jnp.maximum(m_new, jnp.max(s, axis=0, keepdims=True))
            acc = acc_ref[g] * jnp.exp(m_run[g] - m_new)
            ps = [per_head(lambda x, mk=mk: jnp.where(mk, x, 0.0), jnp.exp(s - m_new)).astype(BF16)
                  for s, mk in zip(ss, msks[g])]
            for t, p in zip(tile_ids, ps):
                acc = acc + jnp.dot(vt_refs[g][t], p, preferred_element_type=F32)
            acc_ref[g] = acc
            m_out.append(m_new)
        return tuple(m_out)

    def normalised():
        outs = []
        for g in range(N_NSA_KV):
            acc = acc_ref[g]
            denom = acc[HEAD_DIM:HEAD_DIM + 1, :] if g == 0 else acc[0:1, :]
            outs.append(acc / jnp.maximum(denom, 1e-30))
        return outs

    def reset():
        for g in range(N_NSA_KV):
            acc_ref[g] = jnp.zeros((LANES, cols3), F32)
        return tuple(jnp.full((1, cols3), NEG, F32) for _ in range(N_NSA_KV))

    def sel_tiles(j):
        return [NSA_TILES * j + a for a in range(NSA_TILES)]

    buf_a, buf_b = (raw_a, pk_a), (raw_b, pk_b)

    def sel_issue(j, buf, scores=True):
        raw_ref, pk_ref = buf
        for a, t in enumerate(sel_tiles(j)):
            kt = ks_ref[pl.ds(pl.multiple_of(t * LANES, LANES), LANES), :]
            expand = jnp.where(lax.shift_right_logical(ri, 6) + 2 * t == ci, 1.0, 0.0).astype(BF16)
            for g in groups:
                if scores:
                    raw_ref[g, a] = jnp.dot(kt, q_grp[g], preferred_element_type=F32)
                pk_ref[g, a] = jnp.dot(expand, sel[g], preferred_element_type=F32)

    def sel_consume(j, buf, m_run, diagonal):
        raw_ref, pk_ref = buf
        tiles = sel_tiles(j)
        raw = [[raw_ref[g, a] for a in range(NSA_TILES)] for g in groups]
        msks = []
        for g in groups:
            row = []
            for a, t in enumerate(tiles):
                mk = pk_ref[g, a] > 0.5
                if diagonal:
                    mk = mk & (t * LANES + ri <= tq_lane)
                row.append(mk)
            msks.append(row)
        return softmax_update(raw, msks, (vst0_ref, vst1_ref), tiles, m_run)

    last = qi // NSA_TILES
    sel_issue(0, buf_a, scores=False)

    def sel_body(i, m_run):
        sel_issue(2 * i + 1, buf_b)
        m_run = sel_consume(2 * i, buf_a, m_run, False)
        sel_issue(2 * i + 2, buf_a)
        return sel_consume(2 * i + 1, buf_b, m_run, False)

    m_sel = lax.fori_loop(0, last // 2, sel_body, reset())

    def odd_tail(m_run):
        sel_issue(last, buf_b)
        m_run = sel_consume(last - 1, buf_a, m_run, False)
        sel_consume(last, buf_b, m_run, True)
        return 0

    def even_tail(m_run):
        sel_consume(last, buf_a, m_run, True)
        return 0

    lax.cond((last & 1) == 1, odd_tail, even_tail, m_sel)
    o_s = normalised()
    m0 = reset()
    msk_w = []
    for t, tc in zip(w_tiles, w_clamped):
        diff = tq_lane - (jnp.where(t >= 0, tc * LANES, 1 << 30) + ri)
        msk_w.append((diff >= 0) & (diff < WINDOW))
    softmax_update([[raw_w[g, a] for a in range(len(w_tiles))] for g in groups], [msk_w] * N_NSA_KV,
                   (vwt0_ref, vwt1_ref), w_clamped, m0)
    o_w = normalised()

    gate_t = jax.nn.sigmoid(s_ref[...]).T
    nw_col = nw_ref[...]
    for r in range(NSA_REP):
        cs = slice(r * qb, (r + 1) * qb)
        comb = []
        for g in range(N_NSA_KV):
            h = g * NSA_REP + r
            gc, gs, gw = (gate_t[SMALL_GATE + br * N_NSA + h: SMALL_GATE + br * N_NSA + h + 1, :] for br in range(3))
            comb.append(gc * o_c[g][:, cs] + gs * o_s[g][:, cs] + gw * o_w[g][:, cs])
        o_t = jnp.where(top, comb[0], comb[1])
        sq = o_t * o_t
        ms0 = jnp.sum(jnp.where(top, sq, 0.0), axis=0, keepdims=True)
        ms1 = jnp.sum(jnp.where(top, 0.0, sq), axis=0, keepdims=True)
        o_t = o_t * lax.rsqrt(jnp.where(top, ms0, ms1) * (1.0 / HEAD_DIM) + EPS) * nw_col
        o_ref[:, r * LANES:(r + 1) * LANES] = o_t.T


def _nsa(q_roped, proj, kcmp, vcmp_t, ks, vst0, vst1, kw, vwt0, vwt1, ovl_t, norm_col, bsz, t_len):
    n = proj.shape[0]
    nq = t_len // Q_BLOCK
    n_cmp = t_len // CMP_STRIDE
    per_b = lambda rows: pl.BlockSpec((rows, LANES), lambda b, i: (b, 0))
    tiles = pl.BlockSpec((t_len // LANES, LANES, LANES), lambda b, i: (b, 0, 0))
    return pl.pallas_call(
        _nsa_kernel,
        grid=(bsz, nq),
        in_specs=[pl.BlockSpec((Q_BLOCK, 3 * LANES), lambda b, i: (b * nq + i, 0)),
                  pl.BlockSpec((Q_BLOCK, LANES), lambda b, i: (b * nq + i, CB_SMALL)),
                  per_b(n_cmp), pl.BlockSpec((LANES, n_cmp), lambda b, i: (b, 0)),
                  per_b(t_len), tiles, tiles, per_b(t_len), tiles, tiles,
                  pl.BlockSpec(ovl_t.shape, lambda b, i: (0, 0)),
                  pl.BlockSpec((LANES, 1), lambda b, i: (0, 0))],
        out_specs=pl.BlockSpec((Q_BLOCK, 3 * LANES), lambda b, i: (b * nq + i, 0)),
        out_shape=jax.ShapeDtypeStruct((n, D_NSA), F32),
        scratch_shapes=[pltpu.VMEM((N_NSA_KV, LANES, NSA_REP * Q_BLOCK), F32),
                        pltpu.VMEM((N_NSA_KV, NSA_TILES, LANES, NSA_REP * Q_BLOCK), F32),
                        pltpu.VMEM((N_NSA_KV, NSA_TILES, LANES, NSA_REP * Q_BLOCK), F32),
                        pltpu.VMEM((N_NSA_KV, NSA_TILES, LANES, Q_BLOCK), F32),
                        pltpu.VMEM((N_NSA_KV, NSA_TILES, LANES, Q_BLOCK), F32),
                        pltpu.VMEM((N_NSA_KV, WINDOW // LANES + 1, LANES, NSA_REP * Q_BLOCK), F32)],
        compiler_params=_cparams(2, 48),
        name="nsa",
    )(q_roped, proj, kcmp, vcmp_t, ks, vst0, vst1, kw, vwt0, vwt1, ovl_t, norm_col)


def _sb_kernel(q_ref, k_ref, v_ref, nw_ref, o_ref):
    qi = pl.program_id(1)
    qb = Q_BLOCK
    n_pairs = N_SB // 2
    ri = _iota((qb, LANES), 0)
    ci = _iota((qb, LANES), 1)
    lo = ci < HEAD_DIM
    upper_ones = jnp.concatenate([jnp.where(ri > ci, 1.0, 0.0), jnp.ones((qb, LANES), F32)], axis=1).astype(BF16)
    diag_strict = ci < ri
    zero_bf = jnp.zeros((qb, LANES), BF16)
    q_heads = []
    for p in range(n_pairs):
        q = q_ref[:, p * LANES:(p + 1) * LANES] * BF16(HEAD_DIM ** -0.5)
        q_heads += [jnp.where(lo if j == 0 else jnp.logical_not(lo), q, zero_bf) for j in range(2)]
    n_heads = len(q_heads)

    def tile_pair(jt, carries, accs, first_on_diag, n_slots=2):
        units = []
        vts = {}
        for slot in range(n_slots):
            t = jt - slot
            ok = t >= 0
            k0 = pl.multiple_of(jnp.maximum(t, 0) * LANES, LANES)
            keep = jnp.where(ok, 1.0, 0.0)
            for p in range(n_pairs):
                kt = k_ref[pl.ds(k0, LANES), p * LANES:(p + 1) * LANES]
                vts[slot, p] = v_ref[pl.ds(k0, LANES), p * LANES:(p + 1) * LANES]
                for j in range(2):
                    h = 2 * p + j
                    units.append((h, slot, keep, lax.dot_general(q_heads[h], kt, NT_DIMS, preferred_element_type=F32)))
        log_1ms, sums = [], []
        for j, slot, keep, z in units:
            log_1m = -(jnp.maximum(z, 0.0) + jnp.log1p(jnp.exp(-jnp.abs(z)))) * keep
            if first_on_diag and slot == 0:
                log_1m = jnp.where(diag_strict, log_1m, 0.0)
            log_1ms.append(log_1m)
            sums.append(_dot_exact01_rhs(log_1m, upper_ones))
        carries, accs = list(carries), list(accs)
        weights = []
        for (j, slot, keep, z), log_1m, sm in zip(units, log_1ms, sums):
            a = jnp.exp(z + log_1m + sm[:, :LANES] + carries[j]) * keep
            if first_on_diag and slot == 0:
                a = jnp.where(diag_strict, a, 0.0)
            weights.append(a.astype(BF16))
            carries[j] = carries[j] + sm[:, LANES:]
        for (j, slot, _, _), a in zip(units, weights):
            accs[j] = accs[j] + jnp.dot(a, vts[slot, j // 2], preferred_element_type=F32)
        return tuple(carries), tuple(accs)

    zeros = (jnp.zeros((qb, LANES), F32),) * n_heads
    carries, accs = tile_pair(qi, zeros, zeros, True, n_slots=SB_FIRST_TILES)

    def cond(c):
        jt, carries_, _ = c
        worst = carries_[0]
        for c_h in carries_[1:]:
            worst = jnp.maximum(worst, c_h)
        return jnp.logical_and(jt >= 0, jnp.max(worst) > EXP_UNDERFLOW)

    def body(c):
        jt, carries_, accs_ = c
        carries_, accs_ = tile_pair(jt, carries_, accs_, False)
        return jt - 2, carries_, accs_

    _, _, outs = lax.while_loop(cond, body, (qi - SB_FIRST_TILES, carries, accs))
    for p in range(n_pairs):
        o_ref[:, p * LANES:(p + 1) * LANES] = _head_rmsnorm(jnp.where(lo, outs[2 * p], outs[2 * p + 1]), nw_ref[...], lo)


def _sb(sq, sk, sv, norm_row, bsz, t_len):
    n = sq.shape[0]
    nq = t_len // Q_BLOCK
    return pl.pallas_call(
        _sb_kernel,
        grid=(bsz, nq),
        in_specs=[pl.BlockSpec((Q_BLOCK, D_SB), lambda b, i: (b * nq + i, 0)),
                  pl.BlockSpec((t_len, D_SB), lambda b, i: (b, 0)),
                  pl.BlockSpec((t_len, D_SB), lambda b, i: (b, 0)),
                  pl.BlockSpec((1, LANES), lambda b, i: (0, 0))],
        out_specs=pl.BlockSpec((Q_BLOCK, D_SB), lambda b, i: (b * nq + i, 0)),
        out_shape=jax.ShapeDtypeStruct((n, D_SB), F32),
        compiler_params=_cparams(2, 32),
        name="sb",
    )(sq, sk, sv, norm_row)


def _mix_kernel(h_ref, og_ref, on_ref, os_ref, wg_ref, wn_ref, ws_ref, o_ref):
    acc = h_ref[...]
    acc = acc + jnp.dot(og_ref[...].astype(BF16), wg_ref[...], preferred_element_type=F32)
    acc = acc + jnp.dot(on_ref[...].astype(BF16), wn_ref[...], preferred_element_type=F32)
    acc = acc + jnp.dot(os_ref[...].astype(BF16), ws_ref[...], preferred_element_type=F32)
    o_ref[...] = acc


def _mix(h, o_gdn, o_nsa, o_sb, wg, wn, ws, tm=512):
    n, d = h.shape
    rows = lambda a: pl.BlockSpec((tm, a.shape[1]), lambda i: (i, 0))
    full = lambda a: pl.BlockSpec(a.shape, lambda i: (0, 0))
    return pl.pallas_call(
        _mix_kernel,
        grid=(n // tm,),
        in_specs=[rows(h), rows(o_gdn), rows(o_nsa), rows(o_sb), full(wg), full(wn), full(ws)],
        out_specs=rows(h),
        out_shape=jax.ShapeDtypeStruct((n, d), F32),
        compiler_params=_cparams(1, 40),
        name="mix",
    )(h, o_gdn, o_nsa, o_sb, wg, wn, ws)


def _ffn_kernel(h_ref, halo_ref, res_ref, g_ref, wg_ref, wu_ref, cg_ref, cu_ref, wd_ref, o_ref, xn_ref, acc_ref, *,
                t_len):
    i = pl.program_id(0)
    j = pl.program_id(1)
    tm = h_ref.shape[0]
    pad = halo_ref.shape[0]

    @pl.when(j == 0)
    def _():
        halo_keep = jnp.where((i * tm) % t_len == 0, 0.0, 1.0)
        xn_ref[pl.ds(0, pad), :] = (_rmsnorm_rows(halo_ref[...], g_ref[...]) * halo_keep).astype(BF16)
        xn_ref[pl.ds(pad, tm), :] = _rmsnorm_rows(h_ref[...], g_ref[...]).astype(BF16)
        acc_ref[...] = jnp.zeros(acc_ref.shape, F32)

    xn = xn_ref[...]

    def conv(w_ref, c_ref):
        u = jnp.dot(xn, w_ref[...], preferred_element_type=F32)
        out = None
        for s in range(FFN_CONV):
            term = u[pad - (FFN_CONV - 1) + s: pad - (FFN_CONV - 1) + s + tm, :] * c_ref[s:s + 1, :]
            out = term if out is None else out + term
        return out

    act = _silu(conv(wg_ref, cg_ref)) * conv(wu_ref, cu_ref)
    acc_ref[...] += jnp.dot(act.astype(BF16), wd_ref[...], preferred_element_type=F32)

    @pl.when(j == pl.num_programs(1) - 1)
    def _():
        o_ref[...] = res_ref[...] + acc_ref[...]


def _ffn(h, res, gain, w_up, conv_w, w_down, t_len, f0, tm=512, tf=FFN_TF):
    n, d = h.shape
    d_ff = w_down.shape[0]
    nf = d_ff // tf
    pad = 16
    return pl.pallas_call(
        functools.partial(_ffn_kernel, t_len=t_len),
        grid=(n // tm, 1),
        in_specs=[pl.BlockSpec((tm, d), lambda i, j: (i, 0)),
                  pl.BlockSpec((pad, d), lambda i, j: (jnp.maximum(i * (tm // pad) - 1, 0), 0)),
                  pl.BlockSpec((tm, d), lambda i, j: (i, 0)),
                  pl.BlockSpec((1, d), lambda i, j: (0, 0)),
                  pl.BlockSpec((d, tf), lambda i, j: (0, f0)),
                  pl.BlockSpec((d, tf), lambda i, j: (0, nf + f0)),
                  pl.BlockSpec((FFN_CONV, tf), lambda i, j: (0, f0)),
                  pl.BlockSpec((FFN_CONV, tf), lambda i, j: (0, nf + f0)),
                  pl.BlockSpec((tf, d), lambda i, j: (f0, 0))],
        out_specs=pl.BlockSpec((tm, d), lambda i, j: (i, 0)),
        out_shape=jax.ShapeDtypeStruct((n, d), F32),
        scratch_shapes=[pltpu.VMEM((tm + pad, d), BF16), pltpu.VMEM((tm, d), F32)],
        compiler_params=_cparams(2, 56),
        name="ffn",
    )(h, h, res, gain, w_up, w_up, conv_w, conv_w, w_down)


def _ple_kernel(h_ref, p_ref, g_ref, wg_ref, wp_ref, pn_ref, fin_ref, o_ref, *, final_norm):
    h = h_ref[...]
    gate = jax.nn.sigmoid(jnp.dot(_rmsnorm_rows(h, g_ref[...]).astype(BF16), wg_ref[...], preferred_element_type=F32))
    emb = jnp.dot(p_ref[...].astype(BF16), wp_ref[...], preferred_element_type=F32)
    out = h + gate * _rmsnorm_rows(emb, pn_ref[...])
    if final_norm:
        out = _rmsnorm_rows(out, fin_ref[...])
    o_ref[...] = out


def _ple(h, p_all, layer, gain, w_gate, w_ple, ple_gain, fin_gain, final_norm, tm=512):
    n, d = h.shape
    rows = lambda a: pl.BlockSpec((tm, a.shape[1]), lambda i: (i, 0))
    full = lambda a: pl.BlockSpec(a.shape, lambda i: (0, 0))
    p_spec = pl.BlockSpec((None, tm, p_all.shape[2]), lambda i: (layer, i, 0))
    return pl.pallas_call(
        functools.partial(_ple_kernel, final_norm=final_norm),
        grid=(n // tm,),
        in_specs=[rows(h), p_spec, full(gain), full(w_gate), full(w_ple), full(ple_gain), full(fin_gain)],
        out_specs=rows(h),
        out_shape=jax.ShapeDtypeStruct((n, d), F32),
        compiler_params=_cparams(1, 40),
        name="ple",
    )(h, p_all, gain, w_gate, w_ple, ple_gain, fin_gain)


def _overlap_t(t_len):
    n_cmp = t_len // CMP_STRIDE
    n_sel = t_len // SEL_BLOCK
    c0 = np.arange(n_cmp) * CMP_STRIDE
    s0 = np.arange(n_sel) * SEL_BLOCK
    ov = np.clip(np.minimum(c0[None, :] + CMP_LEN, s0[:, None] + SEL_BLOCK) - np.maximum(c0[None, :], s0[:, None]), 0, None)
    out = np.zeros((LANES, n_cmp), np.float32)
    out[:n_sel] = ov.astype(np.float32) / CMP_LEN
    return jnp.asarray(out)


def _tile_row(v, reps):
    return jnp.tile(v.astype(F32), reps).reshape(1, -1)


def kernel(x, p, positions, ln_mix, w_in, gdn_conv, gdn_a_log, gdn_dt_bias, gdn_norm, nsa_pe_k, nsa_pe_v, nsa_cmp_k_w1, nsa_cmp_k_w2, nsa_cmp_v_w1, nsa_cmp_v_w2, nsa_norm, sb_norm, w_out, ln_ffn, w_up, ffn_conv, w_down, ln_ple, w_ple_gate, w_ple, ple_norm, ln_final):
    bsz, t_len, d_model = x.shape
    depth = w_in.shape[0]
    n = bsz * t_len
    assert t_len % (NSA_TILES * LANES) == 0 and 2 < t_len // SEL_BLOCK <= LANES

    cols = _inproj_columns()
    mix_rows = _mix_rows()
    ovl_t = _overlap_t(t_len)

    half = HEAD_DIM // 2
    inv = ROPE_THETA ** (-jnp.arange(half, dtype=F32) / half)
    inv_row = jnp.tile(inv, LANES // half).reshape(1, LANES)
    sgn_row = jnp.asarray(np.where((np.arange(LANES) % HEAD_DIM) < half, -1.0, 1.0).astype(np.float32)).reshape(1, LANES)
    cos_t, sin_t = _rope_tables(positions.reshape(n, 1), inv_row, sgn_row)

    pad_lanes = lambda v: jnp.pad(v.astype(F32), (0, LANES - v.shape[0])).reshape(1, LANES)
    h = x.reshape(n, d_model)
    for i in range(depth):
        w_in_p = _permute_static(w_in[i].astype(BF16), cols, axis=1)
        proj = _inproj(h, ln_mix[i].reshape(1, d_model), w_in_p)

        o_gdn = _gdn(proj, gdn_conv[i], pad_lanes(gdn_a_log[i]), pad_lanes(gdn_dt_bias[i]),
                     _tile_row(gdn_norm[i], 2), bsz, t_len)

        q_r, kc_r, ks_r, kw_r, vst0, vst1, vwt0, vwt1, sq_b, sk_b, sv_b = _prep(proj, cos_t, sin_t)
        w1dup = lambda w1: jnp.tile(w1.reshape(CMP_LEN, HEAD_DIM, -1), (1, 2, 1)).astype(BF16)
        w2pad = lambda w2: jnp.stack([jnp.pad(w2, ((0, 0), (0, HEAD_DIM))), jnp.pad(w2, ((0, 0), (HEAD_DIM, 0)))]).astype(BF16)
        kcmp, vcmp = _compress(kc_r, proj, jnp.tile(nsa_pe_k[i], (1, 2)), jnp.tile(nsa_pe_v[i], (1, 2)),
                               w1dup(nsa_cmp_k_w1[i]), w1dup(nsa_cmp_v_w1[i]),
                               w2pad(nsa_cmp_k_w2[i]), w2pad(nsa_cmp_v_w2[i]), bsz, t_len)
        o_nsa = _nsa(q_r, proj, kcmp, vcmp, ks_r, vst0, vst1, kw_r, vwt0, vwt1, ovl_t,
                     _tile_row(nsa_norm[i], 2).reshape(LANES, 1), bsz, t_len)
        o_sb = _sb(sq_b, sk_b, sv_b, _tile_row(sb_norm[i], 2), bsz, t_len)

        w_o = _permute_static(w_out[i].astype(BF16), mix_rows, axis=0)
        h = _mix(h, o_gdn, o_nsa, o_sb, w_o[:D_GDN], w_o[D_GDN:D_GDN + D_NSA], w_o[D_GDN + D_NSA:])

        w_up_b, w_down_b = w_up[i].astype(BF16), w_down[i].astype(BF16)
        part = h
        for f0 in range(w_down_b.shape[0] // FFN_TF):
            part = _ffn(h, part, ln_ffn[i].reshape(1, d_model), w_up_b, ffn_conv[i], w_down_b, t_len, f0)
        h = part

        h = _ple(h, p.reshape(depth, n, -1), i, ln_ple[i].reshape(1, d_model), w_ple_gate[i].astype(BF16),
                 w_ple[i].astype(BF16), ple_norm[i].reshape(1, d_model), ln_final.reshape(1, d_model),
                 final_norm=(i == depth - 1))
    return h.reshape(bsz, t_len, d_model)
```
